```python
import jax
import jax.numpy as jnp
from jax import lax
import numpy as np

D_MODEL = 1024
BATCH = 32
SEQ = 2048
DEPTH = 2

CTX_LEN = 256
GRID_W = 64
HEAD_DIM = 64
N_GROUPS = 4
GROUP_W = D_MODEL // N_GROUPS
RET_HEADS = GROUP_W // HEAD_DIM
GQA_HEADS = GROUP_W // HEAD_DIM
GQA_KV_HEADS = GQA_HEADS // 2
SWA_HEADS = GROUP_W // HEAD_DIM
SWA_KV_HEADS = SWA_HEADS // 2
HGRN_HEADS = GROUP_W // HEAD_DIM
HGRN_EXPAND = 64
HGRN_FDIM = HGRN_HEADS * HGRN_EXPAND
D_FF = 4 * D_MODEL
Q_BLOCK = 128
WINDOW = 128
RET_CHUNK = 128
HGRN_CHUNK = 64
ROPE_THETA = 10000.0
ROPE_FREQS = HEAD_DIM // 4
ALPHA = (2 * DEPTH) ** 0.25
BETA = (8 * DEPTH) ** -0.25
EPS = 1e-6
NEG = -1e30
TINY = 1e-30
IN_SIZES = (GROUP_W, GROUP_W, GROUP_W, GROUP_W,
            GQA_HEADS * HEAD_DIM, GQA_KV_HEADS * HEAD_DIM, GQA_KV_HEADS * HEAD_DIM,
            SWA_HEADS * HEAD_DIM, SWA_KV_HEADS * HEAD_DIM, SWA_KV_HEADS * HEAD_DIM,
            HGRN_FDIM, HGRN_FDIM, HGRN_FDIM, GROUP_W, GROUP_W)
N_IN = sum(IN_SIZES)
F32 = jnp.float32

kernel_name = "hybrid_dit_retention_gqa_swa_hgrn2"


def _ln(x):
    x32 = x.astype(F32)
    mu = jnp.mean(x32, axis=-1, keepdims=True)
    var = jnp.mean(jnp.square(x32 - mu), axis=-1, keepdims=True)
    return ((x32 - mu) * lax.rsqrt(var + EPS)).astype(x.dtype)


def _rms(x):
    x32 = x.astype(F32)
    return (x32 * lax.rsqrt(jnp.mean(jnp.square(x32), axis=-1, keepdims=True) + EPS)).astype(x.dtype)


def _rope_tables(n):
    rows = n // GRID_W
    row = jnp.repeat(jnp.arange(rows), GRID_W).astype(F32)
    col = (jnp.arange(rows * GRID_W) % GRID_W).astype(F32)
    inv = ROPE_THETA ** (-jnp.arange(ROPE_FREQS, dtype=F32) / ROPE_FREQS)
    ang = jnp.stack([row[:, None] * inv, col[:, None] * inv], axis=0)
    return jnp.cos(ang), jnp.sin(ang)


def _rope(x, cos, sin):
    xs = x.reshape(*x.shape[:-1], 2, 2, ROPE_FREQS)
    c = jnp.moveaxis(cos, 0, 1)[None, :, None].astype(x.dtype)
    s = jnp.moveaxis(sin, 0, 1)[None, :, None].astype(x.dtype)
    x1, x2 = xs[..., 0, :], xs[..., 1, :]
    return jnp.stack([x1 * c - x2 * s, x2 * c + x1 * s], axis=-2).reshape(x.shape)


def _attend(q, k, v, mask=None, sink=None):
    s = jnp.einsum('btkgd,bskd->bkgts', q, k).astype(F32) * (HEAD_DIM ** -0.5)
    if mask is not None:
        s = jnp.where(mask, s, NEG)
    if sink is None:
        p = jax.nn.softmax(s, axis=-1)
    else:
        sk = sink.astype(F32)[None, :, :, None, None]
        m = jnp.maximum(jnp.max(s, axis=-1, keepdims=True), sk)
        e = jnp.exp(s - m)
        p = e / (jnp.sum(e, axis=-1, keepdims=True) + jnp.exp(sk - m))
    return jnp.einsum('bkgts,bskd->btkgd', p.astype(v.dtype), v)


def _chunk_recurrence(q, k, v, log_a, s0, chunk, with_output):
    B, H, L, _ = q.shape
    dv = v.shape[-1]
    n = L // chunk

    def split(t):
        return jnp.moveaxis(t.astype(F32).reshape(t.shape[0], t.shape[1], n, chunk, t.shape[-1]), 2, 0)

    causal = jnp.tril(jnp.ones((chunk, chunk), dtype=bool))[:, :, None]

    def step(S, xs):
        qc, kc, vc, gc = xs
        b = jnp.cumsum(gc, axis=-2)
        b_last = b[..., -1:, :]
        S_new = jnp.exp(jnp.swapaxes(b_last, -1, -2)) * S + jnp.einsum(
            'bhsd,bhse->bhde', kc * jnp.exp(b_last - b), vc)
        if not with_output:
            return S_new, None
        o_inter = jnp.einsum('bhtd,bhde->bhte', qc * jnp.exp(b), S)
        diff = b[..., :, None, :] - b[..., None, :, :]
        decay = jnp.where(causal, jnp.exp(jnp.where(causal, diff, 0.0)), 0.0)
        if gc.shape[-1] == 1:
            att = jnp.einsum('bhtd,bhsd->bhts', qc, kc) * decay[..., 0]
        else:
            att = jnp.einsum('bhtd,bhsd,bhtsd->bhts', qc, kc, decay)
        return S_new, o_inter + jnp.einsum('bhts,bhse->bhte', att, vc)

    S, os = lax.scan(step, s0.astype(F32), tuple(split(t) for t in (q, k, v, log_a)))
    if not with_output:
        return None, S
    return jnp.moveaxis(os, 0, 2).reshape(B, H, L, dv).astype(v.dtype), S


def _bidir_scan(q, ks, v, las, qc, kcs, vc, lacs, chunk, ctx_out):
    B, H, _, dk = q.shape
    dv = v.shape[-1]
    o_lat, o_ctx = 0.0, 0.0
    for d in range(2):
        fl = (lambda t: jnp.flip(t, axis=2)) if d == 1 else (lambda t: t)
        s0 = jnp.zeros((B, H, dk, dv), F32)
        oc, s_ctx = _chunk_recurrence(fl(qc), fl(kcs[d]), fl(vc), fl(lacs[d]), s0, chunk, ctx_out)
        o, _ = _chunk_recurrence(fl(q), fl(ks[d]), fl(v), fl(las[d]), s_ctx, chunk, True)
        o_lat = o_lat + fl(o)
        if ctx_out:
            o_ctx = o_ctx + fl(oc)
    return o_lat, (o_ctx if ctx_out else None)


def _retention(q, k, v, g, qc, kc, vc, gc, decay_logit, cos, sin, ctx_out):
    H = RET_HEADS

    def prep(t, rot):
        t = t.reshape(t.shape[0], t.shape[1], H, HEAD_DIM)
        if rot:
            t = _rope(t, cos, sin)
        return t.transpose(0, 2, 1, 3)

    kscale = HEAD_DIM ** -0.5
    q_, k_, v_ = prep(q, True), prep(k, True) * kscale, prep(v, False)
    qc_, kc_, vc_ = prep(qc, False), prep(kc, False) * kscale, prep(vc, False)
    L, Lc = q.shape[1], qc.shape[1]
    log_g = jax.nn.log_sigmoid(decay_logit.astype(F32))
    las = [jnp.broadcast_to(log_g[d][None, :, None, None], (1, H, L, 1)) for d in range(2)]
    lacs = [jnp.broadcast_to(log_g[d][None, :, None, None], (1, H, Lc, 1)) for d in range(2)]
    o, oc = _bidir_scan(q_, [k_, k_], v_, las, qc_, [kc_, kc_], vc_, lacs, RET_CHUNK, ctx_out)

    def finish(o, gate):
        on = _ln(o).transpose(0, 2, 1, 3).reshape(gate.shape).astype(gate.dtype)
        return on * jax.nn.silu(gate)

    return finish(o, g), (finish(oc, gc) if ctx_out else None)


def _global_gqa(q, k, v, qc, kc, vc, q_gain, k_gain, cos, sin, ctx_out):
    B, L, _ = q.shape
    Lc = qc.shape[1]
    KV, G = GQA_KV_HEADS, GQA_HEADS // GQA_KV_HEADS

    def qk(t, nh, gain):
        return _rms(t.reshape(t.shape[0], t.shape[1], nh, HEAD_DIM)) * gain

    q_ = _rope(qk(q, GQA_HEADS, q_gain), cos, sin)
    k_ = _rope(qk(k, KV, k_gain), cos, sin)
    v_ = v.reshape(B, L, KV, HEAD_DIM)
    kc_ = qk(kc, KV, k_gain)
    vc_ = vc.reshape(B, Lc, KV, HEAD_DIM)
    k_all = jnp.concatenate([kc_, k_], axis=1)
    v_all = jnp.concatenate([vc_, v_], axis=1)
    nb = L // Q_BLOCK
    qb = jnp.moveaxis(q_.reshape(B, nb, Q_BLOCK, KV, G, HEAD_DIM), 1, 0)
    out = lax.map(lambda qi: _attend(qi, k_all, v_all), qb)
    out = jnp.moveaxis(out, 0, 1).reshape(B, L, GQA_HEADS * HEAD_DIM)
    if not ctx_out:
        return out, None
    qc_ = qk(qc, GQA_HEADS, q_gain).reshape(B, Lc, KV, G, HEAD_DIM)
    return out, _attend(qc_, kc_, vc_).reshape(B, Lc, GQA_HEADS * HEAD_DIM)


def _window_gqa(q, k, v, qc, kc, vc, sink, cos, sin, ctx_out):
    B, L, _ = q.shape
    Lc = qc.shape[1]
    KV, G = SWA_KV_HEADS, SWA_HEADS // SWA_KV_HEADS
    nb = L // Q_BLOCK
    sink = sink.reshape(KV, G)
    q_ = _rope(q.reshape(B, L, SWA_HEADS, HEAD_DIM), cos, sin).reshape(B, nb, Q_BLOCK, KV, G, HEAD_DIM)
    k_ = _rope(k.reshape(B, L, KV, HEAD_DIM), cos, sin)
    v_ = v.reshape(B, L, KV, HEAD_DIM)
    kc_ = kc.reshape(B, Lc, KV, HEAD_DIM)
    vc_ = vc.reshape(B, Lc, KV, HEAD_DIM)

    def band(t):
        tp = jnp.pad(t, ((0, 0), (Q_BLOCK, Q_BLOCK), (0, 0), (0, 0))).reshape(B, nb + 2, Q_BLOCK, KV, HEAD_DIM)
        return jnp.concatenate([tp[:, :-2], tp[:, 1:-1], tp[:, 2:]], axis=2)

    blk = jnp.arange(nb)[:, None, None]
    t_pos = blk * Q_BLOCK + jnp.arange(Q_BLOCK)[None, :, None]
    s_pos = (blk - 1) * Q_BLOCK + jnp.arange(3 * Q_BLOCK)[None, None, :]
    valid = (jnp.abs(t_pos - s_pos) <= WINDOW) & (s_pos >= 0) & (s_pos < L)
    mask = jnp.concatenate([jnp.ones((nb, Q_BLOCK, Lc), dtype=bool), valid], axis=-1)

    def block(args):
        qi, ki, vi, mi = args
        return _attend(qi, jnp.concatenate([kc_, ki], axis=1), jnp.concatenate([vc_, vi], axis=1), mi, sink)

    xs = (jnp.moveaxis(q_, 1, 0), jnp.moveaxis(band(k_), 1, 0), jnp.moveaxis(band(v_), 1, 0), mask)
    out = jnp.moveaxis(lax.map(block, xs), 0, 1).reshape(B, L, SWA_HEADS * HEAD_DIM)
    if not ctx_out:
        return out, None
    qc_ = qc.reshape(B, Lc, KV, G, HEAD_DIM)
    return out, _attend(qc_, kc_, vc_, None, sink).reshape(B, Lc, SWA_HEADS * HEAD_DIM)


def _hgrn2(q, zf, zb, i, g, qc, zfc, zbc, ic, gc, lb, ctx_out):
    H, E = HGRN_HEADS, HGRN_EXPAND
    lb = lb.reshape(H, E)

    def heads(t, dh):
        return t.reshape(t.shape[0], t.shape[1], H, dh).transpose(0, 2, 1, 3)

    def gate(z):
        z = z.reshape(z.shape[0], z.shape[1], H, E).astype(F32)
        f = lb + (1.0 - lb) * jax.nn.sigmoid(z)
        log_f = jnp.log(jnp.maximum(f, TINY))
        key = (1.0 - lb) * jax.nn.sigmoid(-z)
        return key.transpose(0, 2, 1, 3), log_f.transpose(0, 2, 1, 3)

    q_, v_ = heads(jax.nn.silu(q), E), heads(i, HEAD_DIM)
    qc_, vc_ = heads(jax.nn.silu(qc), E), heads(ic, HEAD_DIM)
    kf, laf = gate(zf)
    kb, lab = gate(zb)
    kfc, lafc = gate(zfc)
    kbc, labc = gate(zbc)
    o, oc = _bidir_scan(q_, [kf, kb], v_, [laf, lab], qc_, [kfc, kbc], vc_, [lafc, labc], HGRN_CHUNK, ctx_out)

    def finish(o, gt):
        return _rms(o).transpose(0, 2, 1, 3).reshape(gt.shape).astype(gt.dtype) * jax.nn.silu(gt)

    return finish(o, g), (finish(oc, gc) if ctx_out else None)


def _modulation(cvec, w_ada, b_ada):
    return jnp.split(jax.nn.silu(cvec) @ w_ada + b_ada, 6, axis=-1)


def _modulate(x, shift, scale):
    return _ln(x) * (1.0 + scale) + shift


def _post(x, y, gate, gain, bias):
    return _ln(ALPHA * x + gate * y) * gain + bias


def _sq_relu_mlp(h, w_up, w_down):
    return jnp.square(jax.nn.relu(h @ w_up)) @ w_down


def setup_inputs(seed: int = 0) -> dict:
    key = jax.random.key(seed)
    ks = jax.random.split(key, 20)

    def nrm(k, shape, s=1.0):
        return jax.random.normal(k, shape, F32) * s

    base_logit = jnp.log(2.0 ** (5.0 + jnp.arange(RET_HEADS, dtype=F32)) - 1.0)
    return {
        "x": nrm(ks[0], (BATCH, SEQ, D_MODEL)),
        "c": nrm(ks[1], (BATCH, D_MODEL)),
        "ctx": nrm(ks[2], (BATCH, CTX_LEN, D_MODEL)),
        "c_ctx": nrm(ks[3], (D_MODEL,)),
        "w_ada": nrm(ks[4], (DEPTH, D_MODEL, 6 * D_MODEL), 0.5 * D_MODEL ** -0.5),
        "b_ada": nrm(ks[5], (DEPTH, 6 * D_MODEL), 0.02),
        "w_in": nrm(ks[6], (DEPTH, D_MODEL, N_IN), D_MODEL ** -0.5),
        "ret_decay_logit": base_logit + nrm(ks[7], (DEPTH, 2, RET_HEADS), 0.1),
        "gqa_q_gain": 1.0 + nrm(ks[8], (DEPTH, HEAD_DIM), 0.02),
        "gqa_k_gain": 1.0 + nrm(ks[9], (DEPTH, HEAD_DIM), 0.02),
        "swa_sink": nrm(ks[10], (DEPTH, SWA_HEADS)),
        "hgrn_lb": nrm(ks[11], (DEPTH, HGRN_FDIM)),
        "w_out": nrm(ks[12], (DEPTH, D_MODEL, D_MODEL), BETA * D_MODEL ** -0.5),
        "ln1_g": 1.0 + nrm(ks[13], (DEPTH, D_MODEL), 0.02),
        "ln1_b": nrm(ks[14], (DEPTH, D_MODEL), 0.02),
        "w_up": nrm(ks[15], (DEPTH, D_MODEL, D_FF), D_MODEL ** -0.5),
        "w_down": nrm(ks[16], (DEPTH, D_FF, D_MODEL), BETA * D_FF ** -0.5),
        "ln2_g": 1.0 + nrm(ks[17], (DEPTH, D_MODEL), 0.02),
        "ln2_b": nrm(ks[18], (DEPTH, D_MODEL), 0.02),
    }


def reference(x, c, ctx, c_ctx, w_ada, b_ada, w_in, ret_decay_logit, gqa_q_gain, gqa_k_gain,
              swa_sink, hgrn_lb, w_out, ln1_g, ln1_b, w_up, w_down, ln2_g, ln2_b):
    L = x.shape[1]
    cos, sin = _rope_tables(L)
    p_lb = jax.nn.softmax(hgrn_lb.astype(F32), axis=0)
    lower_bounds = jnp.cumsum(p_lb, axis=0) - p_lb[0]
    split_at = np.cumsum(IN_SIZES)[:-1].tolist()
    xc = ctx
    for l in range(DEPTH):
        ctx_out = l < DEPTH - 1
        m_lat = [m[:, None, :] for m in _modulation(c, w_ada[l], b_ada[l])]
        m_ctx = _modulation(c_ctx, w_ada[l], b_ada[l])
        h = _modulate(x, m_lat[0], m_lat[1])
        hc = _modulate(xc, m_ctx[0], m_ctx[1])
        p = jnp.split(h @ w_in[l], split_at, axis=-1)
        pc = jnp.split(hc @ w_in[l], split_at, axis=-1)
        y_ret, yc_ret = _retention(*p[0:4], *pc[0:4], ret_decay_logit[l], cos, sin, ctx_out)
        y_glb, yc_glb = _global_gqa(*p[4:7], *pc[4:7], gqa_q_gain[l], gqa_k_gain[l], cos, sin, ctx_out)
        y_win, yc_win = _window_gqa(*p[7:10], *pc[7:10], swa_sink[l], cos, sin, ctx_out)
        y_hg, yc_hg = _hgrn2(*p[10:15], *pc[10:15], lower_bounds[l], ctx_out)
        y = jnp.concatenate([y_ret, y_glb, y_win, y_hg], axis=-1) @ w_out[l]
        x = _post(x, y, m_lat[2], ln1_g[l], ln1_b[l])
        x = _post(x, _sq_relu_mlp(_modulate(x, m_lat[3], m_lat[4]), w_up[l], w_down[l]), m_lat[5], ln2_g[l], ln2_b[l])
        if ctx_out:
            yc = jnp.concatenate([yc_ret, yc_glb, yc_win, yc_hg], axis=-1) @ w_out[l]
            xc = _post(xc, yc, m_ctx[2], ln1_g[l], ln1_b[l])
            xc = _post(xc, _sq_relu_mlp(_modulate(xc, m_ctx[3], m_ctx[4]), w_up[l], w_down[l]), m_ctx[5], ln2_g[l], ln2_b[l])
    return x
```

```python
import functools

import jax
import jax.numpy as jnp
from jax import lax
from jax.experimental import pallas as pl
from jax.experimental.pallas import tpu as pltpu

F32 = jnp.float32
BF16 = jnp.bfloat16

D_MODEL = 1024
CTX_LEN = 256
GRID_W = 64
HEAD_DIM = 64
GROUP_W = 256
D_FF = 4 * D_MODEL
DEPTH = 2
ROPE_THETA = 10000.0
ROPE_FREQS = HEAD_DIM // 4
WINDOW = 128
ALPHA = (2 * DEPTH) ** 0.25
EPS = 1e-6
NEG = -1e30
TINY = 1e-30
QK_SCALE = HEAD_DIM ** -0.5

LANES = 128
CHUNK = 128
N_LEVELS = 7
VMEM_LIMIT = 56 * 1024 * 1024


def _dot(a, b):
    return jnp.dot(a, b, preferred_element_type=F32)


def _dot_nt(a, b):
    return lax.dot_general(a, b, (((1,), (1,)), ((), ())), preferred_element_type=F32)


def _dot_tn(a, b):
    return lax.dot_general(a, b, (((0,), (0,)), ((), ())), preferred_element_type=F32)


def _split_dot(x, w):
    hi = x.astype(BF16)
    lo = (x - hi.astype(F32)).astype(BF16)
    return _dot(hi, w) + _dot(lo, w)


def _silu(x):
    return x * jax.nn.sigmoid(x)


def _ln_rows(x):
    mu = jnp.mean(x, axis=-1, keepdims=True)
    xc = x - mu
    var = jnp.mean(xc * xc, axis=-1, keepdims=True)
    return xc * lax.rsqrt(var + EPS)


def _rope(x, c, s1, s2):
    return x * c + pltpu.roll(x, LANES - ROPE_FREQS, 1) * s1 + pltpu.roll(x, ROPE_FREQS, 1) * s2


def _iota(shape, dim):
    return lax.broadcasted_iota(jnp.int32, shape, dim)


def _head_block_mask():
    return (_iota((LANES, LANES), 0) // HEAD_DIM) == (_iota((LANES, LANES), 1) // HEAD_DIM)


def _stack_heads(qb):
    lo = _iota(qb.shape, 1) < HEAD_DIM
    zero = jnp.zeros_like(qb)
    return jnp.concatenate([jnp.where(lo, qb, zero), jnp.where(lo, zero, qb)], axis=0)


def _unstack_heads(r):
    t = r.shape[0] // 2
    lo = _iota((t, LANES), 1) < HEAD_DIM
    return jnp.where(lo, r[:t], r[t:])


def _mod_kernel(c_ref, w_ref, b_ref, o_ref):
    s = _silu(c_ref[...])
    o_ref[0] = _dot(s.astype(BF16), w_ref[0].astype(BF16)) + b_ref[0]


def _modulation(cvec, w_ada, b_ada):
    rows = cvec.shape[0]
    depth, d, n = w_ada.shape
    nt = n // d
    return pl.pallas_call(
        _mod_kernel,
        grid=(depth, nt),
        in_specs=[pl.BlockSpec((rows, d), lambda l, j: (0, 0)),
                  pl.BlockSpec((1, d, d), lambda l, j: (l, 0, j)),
                  pl.BlockSpec((1, 1, d), lambda l, j: (l, 0, j))],
        out_specs=pl.BlockSpec((1, rows, d), lambda l, j: (l, 0, j)),
        out_shape=jax.ShapeDtypeStruct((depth, rows, n), F32),
        compiler_params=pltpu.CompilerParams(dimension_semantics=("arbitrary", "arbitrary"),
                                             vmem_limit_bytes=VMEM_LIMIT),
        name="modulation",
    )(cvec, w_ada, b_ada.reshape(depth, 1, n))


PROJ_GROUPS = (4 * GROUP_W, 2 * GROUP_W, 2 * GROUP_W, 5 * GROUP_W)
PROJ_COLS = 256


def _proj_kernel(x_ref, mod_ref, w_ref, o_ret, o_glb, o_win, o_hg, h_scr):
    xn = _ln_rows(x_ref[0])
    shift = mod_ref[0, 0:1, :]
    scale = mod_ref[0, 1:2, :]
    h_scr[...] = (xn * (1.0 + scale) + shift).astype(BF16)
    col = 0
    for o_ref, width in zip((o_ret, o_glb, o_win, o_hg), PROJ_GROUPS):
        for j in range(width // PROJ_COLS):
            o_ref[0, :, j * PROJ_COLS:(j + 1) * PROJ_COLS] = _dot(
                h_scr[...], w_ref[:, col:col + PROJ_COLS]).astype(BF16)
            col += PROJ_COLS


def _project(x, mod, w_in_bf, tm, shared_mod):
    b, n, d = x.shape
    n_in = w_in_bf.shape[1]
    mod_map = (lambda i, t: (0, 0, 0)) if shared_mod else (lambda i, t: (i, 0, 0))
    return pl.pallas_call(
        _proj_kernel,
        grid=(b, n // tm),
        in_specs=[pl.BlockSpec((1, tm, d), lambda i, t: (i, t, 0)),
                  pl.BlockSpec((1, 6, d), mod_map),
                  pl.BlockSpec((d, n_in), lambda i, t: (0, 0))],
        out_specs=[pl.BlockSpec((1, tm, w), lambda i, t: (i, t, 0)) for w in PROJ_GROUPS],
        out_shape=[jax.ShapeDtypeStruct((b, n, w), BF16) for w in PROJ_GROUPS],
        scratch_shapes=[pltpu.VMEM((tm, d), BF16)],
        compiler_params=pltpu.CompilerParams(dimension_semantics=("arbitrary", "arbitrary"),
                                             vmem_limit_bytes=VMEM_LIMIT),
        name="adaln_in_proj",
    )(x, mod, w_in_bf)


def _ret_kernel(lg_ref, q_ref, k_ref, v_ref, g_ref, qc_ref, kc_ref, vc_ref, gc_ref,
                cos_ref, s1_ref, s2_ref, *rest, ctx_out):
    if ctx_out:
        y_ref, yc_ref, o_scr, qs_scr, ks_scr = rest
    else:
        y_ref, o_scr, qs_scr, ks_scr = rest
        yc_ref = None
    j = pl.program_id(1)
    c_len = CHUNK
    n_lat = q_ref.shape[1] // c_len
    n_ctx = qc_ref.shape[1] // c_len

    hi1 = _iota((1, LANES), 1) >= HEAD_DIM
    rowi = _iota((c_len, LANES), 0)
    coli = _iota((c_len, LANES), 1)
    pos = rowi.astype(F32)
    lgf = jnp.where(hi1, lg_ref[0, 2 * j + 1], lg_ref[0, 2 * j])
    lgb = jnp.where(hi1, lg_ref[1, 2 * j + 1], lg_ref[1, 2 * j])
    wqf = jnp.exp(lgf * (pos + 1.0))
    wqb = jnp.exp(lgb * (c_len - pos))
    wkf = jnp.exp(lgf * (c_len - 1.0 - pos))
    wkb = jnp.exp(lgb * pos)
    dcf = jnp.exp(lgf * c_len)
    dcb = jnp.exp(lgb * c_len)
    dlt = (rowi - coli).astype(F32)

    def head_mask(e):
        lf = lg_ref[0, 2 * j + e]
        lb = lg_ref[1, 2 * j + e]
        return (jnp.where(dlt >= 0, jnp.exp(lf * jnp.maximum(dlt, 0.0)), 0.0)
                + jnp.where(dlt <= 0, jnp.exp(lb * jnp.maximum(-dlt, 0.0)), 0.0))

    m2 = jnp.concatenate([head_mask(0), head_mask(1)], axis=0)
    bd = _head_block_mask()
    bdm = jnp.where(bd, 1.0 / HEAD_DIM, 0.0).astype(BF16)

    def fwd_step(c, q, k, v, st, need_out):
        k8 = k * QK_SCALE
        if need_out:
            qb = q.astype(BF16)
            a2 = _dot_nt(_stack_heads(qb), k8.astype(BF16))
            r = _dot((a2 * m2).astype(BF16), v)
            o = _unstack_heads(r) + _dot_nt((q * wqf).astype(BF16), st.astype(BF16))
            o_scr[pl.ds(c * c_len, c_len), :] = o
        qs_scr[pl.ds(c * c_len, c_len), :] = q
        ks_scr[pl.ds(c * c_len, c_len), :] = k8
        u = _dot_tn(v, (k8 * wkf).astype(BF16))
        return st * dcf + jnp.where(bd, u, 0.0)

    def bwd_step(c, v, g, st, out_ref, row0, need_out):
        q = qs_scr[pl.ds(c * c_len, c_len), :]
        k8 = ks_scr[pl.ds(c * c_len, c_len), :]
        if need_out:
            o = o_scr[pl.ds(c * c_len, c_len), :] + _dot_nt((q * wqb).astype(BF16), st.astype(BF16))
            mu = _split_dot(o, bdm)
            xc = o - mu
            var = _split_dot(xc * xc, bdm)
            out_ref[0, pl.ds(row0, c_len), :] = (xc * lax.rsqrt(var + EPS) * _silu(g)).astype(out_ref.dtype)
        u = _dot_tn(v, (k8 * wkb).astype(BF16))
        return st * dcb + jnp.where(bd, u, 0.0)

    st = jnp.zeros((LANES, LANES), F32)
    for c in range(n_ctx):
        rows = slice(c * c_len, (c + 1) * c_len)
        st = fwd_step(c, qc_ref[0, rows, :].astype(F32), kc_ref[0, rows, :].astype(F32),
                      vc_ref[0, rows, :], st, ctx_out)

    def lat_fwd(i, st):
        rows = pl.ds(pl.multiple_of(i * c_len, c_len), c_len)
        cs, a1, a2 = cos_ref[rows, :], s1_ref[rows, :], s2_ref[rows, :]
        q = _rope(q_ref[0, rows, :].astype(F32), cs, a1, a2)
        k = _rope(k_ref[0, rows, :].astype(F32), cs, a1, a2)
        return fwd_step(i + n_ctx, q, k, v_ref[0, rows, :], st, True)

    lax.fori_loop(0, n_lat, lat_fwd, st)

    st = jnp.zeros((LANES, LANES), F32)
    for c in reversed(range(n_ctx)):
        rows = slice(c * c_len, (c + 1) * c_len)
        st = bwd_step(c, vc_ref[0, rows, :], gc_ref[0, rows, :].astype(F32) if ctx_out else None,
                      st, yc_ref, c * c_len, ctx_out)

    def lat_bwd(i, st):
        ci = n_lat - 1 - i
        row0 = pl.multiple_of(ci * c_len, c_len)
        rows = pl.ds(row0, c_len)
        return bwd_step(ci + n_ctx, v_ref[0, rows, :], g_ref[0, rows, :].astype(F32), st, y_ref, row0, True)

    lax.fori_loop(0, n_lat, lat_bwd, st)


def _lane_block(n, idx):
    return pl.BlockSpec((1, n, LANES), lambda i, j, idx=idx: (i, 0, idx + j))


def _retention(p, pc, log_decay, rope_tabs, ctx_out):
    b, n, _ = p.shape
    nc = pc.shape[1]
    tab = pl.BlockSpec((n, LANES), lambda i, j: (0, 0))
    out_specs = [pl.BlockSpec((1, n, LANES), lambda i, j: (i, 0, j))]
    out_shape = [jax.ShapeDtypeStruct((b, n, GROUP_W), BF16)]
    if ctx_out:
        out_specs.append(pl.BlockSpec((1, nc, LANES), lambda i, j: (i, 0, j)))
        out_shape.append(jax.ShapeDtypeStruct((b, nc, GROUP_W), BF16))
    res = pl.pallas_call(
        functools.partial(_ret_kernel, ctx_out=ctx_out),
        grid=(b, 2),
        in_specs=[pl.BlockSpec(memory_space=pltpu.SMEM)]
        + [_lane_block(n, 2 * s) for s in range(4)] + [_lane_block(nc, 2 * s) for s in range(4)]
        + [tab, tab, tab],
        out_specs=out_specs,
        out_shape=out_shape,
        scratch_shapes=[pltpu.VMEM((n + nc, LANES), F32)] * 3,
        compiler_params=pltpu.CompilerParams(dimension_semantics=("arbitrary", "arbitrary"),
                                             vmem_limit_bytes=VMEM_LIMIT),
        name="retention",
    )(log_decay, p, p, p, p, pc, pc, pc, pc, *rope_tabs)
    return res if ctx_out else (res[0], None)


def _own_head_dup(x, j):
    own = (_iota(x.shape, 1) // HEAD_DIM) == j
    return jnp.where(own, x, pltpu.roll(x, HEAD_DIM, 1))


def _rms_heads(x, bdm, gain):
    return x * lax.rsqrt(_split_dot(x * x, bdm) + EPS) * gain


def _glb_kernel(q_ref, k_ref, v_ref, qc_ref, kc_ref, vc_ref, qg_ref, kg_ref,
                cos_ref, s1_ref, s2_ref, *rest, ctx_out):
    if ctx_out:
        y_ref, yc_ref, kd_scr, vd_scr = rest
    else:
        y_ref, kd_scr, vd_scr = rest
        yc_ref = None
    j = pl.program_id(1)
    n = q_ref.shape[1]
    nc = qc_ref.shape[1]
    tq = CHUNK
    bdm = jnp.where(_head_block_mask(), 1.0 / HEAD_DIM, 0.0).astype(BF16)
    qg = qg_ref[...]
    kg = kg_ref[...]

    for t in range(nc // tq):
        rows = slice(t * tq, (t + 1) * tq)
        kn = _rms_heads(kc_ref[0, rows, :].astype(F32), bdm, kg)
        kd_scr[rows, :] = _own_head_dup(kn, j).astype(BF16)
        vd_scr[rows, :] = _own_head_dup(vc_ref[0, rows, :].astype(F32), j).astype(BF16)

    def prep(t, carry):
        rows = pl.ds(pl.multiple_of(t * tq, tq), tq)
        dst = pl.ds(pl.multiple_of(nc + t * tq, tq), tq)
        kn = _rms_heads(k_ref[0, rows, :].astype(F32), bdm, kg)
        kn = _rope(kn, cos_ref[rows, :], s1_ref[rows, :], s2_ref[rows, :])
        kd_scr[dst, :] = _own_head_dup(kn, j).astype(BF16)
        vd_scr[dst, :] = _own_head_dup(v_ref[0, rows, :].astype(F32), j).astype(BF16)
        return carry

    lax.fori_loop(0, n // tq, prep, 0)

    def attend(qn, kd, vd):
        s = _dot_nt(_stack_heads((qn * QK_SCALE).astype(BF16)), kd)
        m = jnp.max(s, axis=-1, keepdims=True)
        e = jnp.exp(s - m)
        den = jnp.sum(e, axis=-1, keepdims=True)
        return _unstack_heads(_dot(e.astype(BF16), vd) / den)

    def q_tile(t, carry):
        rows = pl.ds(pl.multiple_of(t * tq, tq), tq)
        qn = _rms_heads(q_ref[0, rows, :].astype(F32), bdm, qg)
        qn = _rope(qn, cos_ref[rows, :], s1_ref[rows, :], s2_ref[rows, :])
        y_ref[0, rows, :] = attend(qn, kd_scr[...], vd_scr[...]).astype(y_ref.dtype)
        return carry

    lax.fori_loop(0, n // tq, q_tile, 0)

    if ctx_out:
        for t in range(nc // tq):
            rows = slice(t * tq, (t + 1) * tq)
            qn = _rms_heads(qc_ref[0, rows, :].astype(F32), bdm, qg)
            yc_ref[0, rows, :] = attend(qn, kd_scr[0:nc, :], vd_scr[0:nc, :]).astype(yc_ref.dtype)


def _global_gqa(p, pc, q_gain, k_gain, rope_tabs, ctx_out):
    b, n, _ = p.shape
    nc = pc.shape[1]
    tab = pl.BlockSpec((n, LANES), lambda i, j: (0, 0))
    gain = pl.BlockSpec((1, LANES), lambda i, j: (0, 0))
    kv = lambda rows, idx: pl.BlockSpec((1, rows, LANES), lambda i, j, idx=idx: (i, 0, idx))
    out_specs = [pl.BlockSpec((1, n, LANES), lambda i, j: (i, 0, j))]
    out_shape = [jax.ShapeDtypeStruct((b, n, GROUP_W), BF16)]
    if ctx_out:
        out_specs.append(pl.BlockSpec((1, nc, LANES), lambda i, j: (i, 0, j)))
        out_shape.append(jax.ShapeDtypeStruct((b, nc, GROUP_W), BF16))
    res = pl.pallas_call(
        functools.partial(_glb_kernel, ctx_out=ctx_out),
        grid=(b, 2),
        in_specs=[_lane_block(n, 0), kv(n, 2), kv(n, 3), _lane_block(nc, 0), kv(nc, 2), kv(nc, 3),
                  gain, gain, tab, tab, tab],
        out_specs=out_specs,
        out_shape=out_shape,
        scratch_shapes=[pltpu.VMEM((n + nc, LANES), BF16)] * 2,
        compiler_params=pltpu.CompilerParams(dimension_semantics=("arbitrary", "arbitrary"),
                                             vmem_limit_bytes=VMEM_LIMIT),
        name="global_gqa",
    )(p, p, p, pc, pc, pc, jnp.tile(q_gain, 2)[None, :], jnp.tile(k_gain, 2)[None, :], *rope_tabs)
    return res if ctx_out else (res[0], None)


def _win_kernel(sink_ref, q_ref, k_ref, v_ref, qc_ref, kc_ref, vc_ref,
                cos_ref, s1_ref, s2_ref, *rest, ctx_out):
    if ctx_out:
        y_ref, yc_ref, kd_scr, vd_scr = rest
    else:
        y_ref, kd_scr, vd_scr = rest
        yc_ref = None
    j = pl.program_id(1)
    n = q_ref.shape[1]
    nc = qc_ref.shape[1]
    tq = CHUNK
    nb = n // tq
    lat0 = nc + tq

    zeros = jnp.zeros((tq, LANES), BF16)
    for r0 in (nc, lat0 + n):
        kd_scr[r0:r0 + tq, :] = zeros
        vd_scr[r0:r0 + tq, :] = zeros
    for t in range(nc // tq):
        rows = slice(t * tq, (t + 1) * tq)
        kd_scr[rows, :] = _own_head_dup(kc_ref[0, rows, :].astype(F32), j).astype(BF16)
        vd_scr[rows, :] = _own_head_dup(vc_ref[0, rows, :].astype(F32), j).astype(BF16)

    def prep(t, carry):
        rows = pl.ds(pl.multiple_of(t * tq, tq), tq)
        dst = pl.ds(pl.multiple_of(lat0 + t * tq, tq), tq)
        kn = _rope(k_ref[0, rows, :].astype(F32), cos_ref[rows, :], s1_ref[rows, :], s2_ref[rows, :])
        kd_scr[dst, :] = _own_head_dup(kn, j).astype(BF16)
        vd_scr[dst, :] = _own_head_dup(v_ref[0, rows, :].astype(F32), j).astype(BF16)
        return carry

    lax.fori_loop(0, nb, prep, 0)

    sk = jnp.where(_iota((2 * tq, 1), 0) < tq, sink_ref[2 * j], sink_ref[2 * j + 1])

    def softmax_pv(parts, sk):
        m = sk
        for s, _ in parts:
            m = jnp.maximum(m, jnp.max(s, axis=-1, keepdims=True))
        den = jnp.exp(sk - m)
        acc = None
        for s, vals in parts:
            e = jnp.exp(s - m)
            den = den + jnp.sum(e, axis=-1, keepdims=True)
            pv = _dot(e.astype(BF16), vals)
            acc = pv if acc is None else acc + pv
        return _unstack_heads(acc / den)

    band_r = _iota((2 * tq, 3 * tq), 0) % tq
    band_c = _iota((2 * tq, 3 * tq), 1)
    off = band_c - band_r
    in_band = (off >= 0) & (off <= 2 * WINDOW)

    def q_tile(t, carry):
        rows = pl.ds(pl.multiple_of(t * tq, tq), tq)
        band = pl.ds(pl.multiple_of(nc + t * tq, tq), 3 * tq)
        qn = _rope(q_ref[0, rows, :].astype(F32), cos_ref[rows, :], s1_ref[rows, :], s2_ref[rows, :])
        q2 = _stack_heads((qn * QK_SCALE).astype(BF16))
        s_ctx = _dot_nt(q2, kd_scr[0:nc, :])
        s_band = _dot_nt(q2, kd_scr[band, :])
        valid = in_band & ((t > 0) | (band_c >= tq)) & ((t < nb - 1) | (band_c < 2 * tq))
        s_band = jnp.where(valid, s_band, NEG)
        y_ref[0, rows, :] = softmax_pv([(s_ctx, vd_scr[0:nc, :]), (s_band, vd_scr[band, :])], sk).astype(y_ref.dtype)
        return carry

    lax.fori_loop(0, nb, q_tile, 0)

    if ctx_out:
        for t in range(nc // tq):
            rows = slice(t * tq, (t + 1) * tq)
            q2 = _stack_heads((qc_ref[0, rows, :].astype(F32) * QK_SCALE).astype(BF16))
            s_ctx = _dot_nt(q2, kd_scr[0:nc, :])
            yc_ref[0, rows, :] = softmax_pv([(s_ctx, vd_scr[0:nc, :])], sk).astype(yc_ref.dtype)


def _window_gqa(p, pc, sink, rope_tabs, ctx_out):
    b, n, _ = p.shape
    nc = pc.shape[1]
    tab = pl.BlockSpec((n, LANES), lambda i, j: (0, 0))
    kv = lambda rows, idx: pl.BlockSpec((1, rows, LANES), lambda i, j, idx=idx: (i, 0, idx))
    out_specs = [pl.BlockSpec((1, n, LANES), lambda i, j: (i, 0, j))]
    out_shape = [jax.ShapeDtypeStruct((b, n, GROUP_W), BF16)]
    if ctx_out:
        out_specs.append(pl.BlockSpec((1, nc, LANES), lambda i, j: (i, 0, j)))
        out_shape.append(jax.ShapeDtypeStruct((b, nc, GROUP_W), BF16))
    res = pl.pallas_call(
        functools.partial(_win_kernel, ctx_out=ctx_out),
        grid=(b, 2),
        in_specs=[pl.BlockSpec(memory_space=pltpu.SMEM),
                  _lane_block(n, 0), kv(n, 2), kv(n, 3), _lane_block(nc, 0), kv(nc, 2), kv(nc, 3),
                  tab, tab, tab],
        out_specs=out_specs,
        out_shape=out_shape,
        scratch_shapes=[pltpu.VMEM((n + nc + 2 * CHUNK, LANES), BF16)] * 2,
        compiler_params=pltpu.CompilerParams(dimension_semantics=("arbitrary", "arbitrary"),
                                             vmem_limit_bytes=VMEM_LIMIT),
        name="window_gqa",
    )(sink, p, p, p, pc, pc, pc, *rope_tabs)
    return res if ctx_out else (res[0], None)


def _mid_rows(b, level, rev):
    c_len = b.shape[0]
    h = 1 << level
    mid_off = h if rev else h - 1
    if 2 * h >= 16:
        pieces = [jnp.broadcast_to(b[m * 2 * h + mid_off:m * 2 * h + mid_off + 1, :], (2 * h, b.shape[1]))
                  for m in range(c_len // (2 * h))]
        return pieces[0] if len(pieces) == 1 else jnp.concatenate(pieces, axis=0)
    r = _iota(b.shape, 0) % (2 * h)
    out = b
    for rr in range(2 * h):
        if rr != mid_off:
            out = jnp.where(r == rr, pltpu.roll(b, (rr - mid_off) % c_len, 0), out)
    return out


def _hg_kernel(lb_ref, q_ref, zf_ref, zb_ref, i_ref, g_ref, qc_ref, zfc_ref, zbc_ref, ic_ref, gc_ref,
               *rest, ctx_out):
    if ctx_out:
        y_ref, yc_ref, o_scr = rest
    else:
        y_ref, o_scr = rest
        yc_ref = None
    c_len = CHUNK
    n_lat = q_ref.shape[1] // c_len
    n_ctx = qc_ref.shape[1] // c_len
    lb = lb_ref[0]
    bd = _head_block_mask()
    bd_ones = jnp.where(bd, 1.0, 0.0).astype(BF16)
    bdm = jnp.where(bd, 1.0 / HEAD_DIM, 0.0).astype(BF16)
    rowi = _iota((c_len, c_len), 0)
    coli = _iota((c_len, c_len), 1)
    tri = (jnp.where(coli <= rowi, 1.0, 0.0).astype(BF16), jnp.where(coli >= rowi, 1.0, 0.0).astype(BF16))
    row1 = _iota((c_len, 1), 0)
    row2 = _iota((2 * c_len, c_len), 0) % c_len
    col2 = _iota((2 * c_len, c_len), 1)

    def step(qraw, z, v, st, rev, need_out):
        z = z.astype(F32)
        f = lb + (1.0 - lb) * jax.nn.sigmoid(z)
        x = jnp.log(jnp.maximum(f, TINY))
        kk = (1.0 - lb) * jax.nn.sigmoid(-z)
        b = _split_dot_lhs(tri[1 if rev else 0], x)
        edge = b[0:1, :] if rev else b[c_len - 1:c_len, :]
        o = None
        if need_out:
            qs = _silu(qraw.astype(F32))
            o = _dot_nt((qs * jnp.exp(b)).astype(BF16), st.astype(BF16))
            o = o + _split_dot(qs * kk, bd_ones) * v.astype(F32)
            a = jnp.zeros((2 * c_len, c_len), F32)
            for level in range(N_LEVELS):
                mid = _mid_rows(b, level, rev)
                is_q = ((row1 >> level) & 1) == (0 if rev else 1)
                e = jnp.exp(jnp.where(is_q, b - mid, mid - b))
                qt = jnp.where(is_q, qs * e, 0.0).astype(BF16)
                kt = jnp.where(is_q, 0.0, kk * e).astype(BF16)
                al = _dot_nt(_stack_heads(qt), kt)
                a = a + jnp.where((row2 >> (level + 1)) == (col2 >> (level + 1)), al, 0.0)
            o = o + _unstack_heads(_dot(a.astype(BF16), v))
        u = _dot_tn(v, (kk * jnp.exp(edge - b)).astype(BF16))
        return o, st * jnp.exp(edge) + jnp.where(bd, u, 0.0)

    def finish(o, g):
        return o * lax.rsqrt(_split_dot(o * o, bdm) + EPS) * _silu(g.astype(F32))

    for rev in (False, True):
        z_ref, zc_ref = (zb_ref, zbc_ref) if rev else (zf_ref, zfc_ref)
        st = jnp.zeros((LANES, LANES), F32)
        for c in (reversed(range(n_ctx)) if rev else range(n_ctx)):
            rows = slice(c * c_len, (c + 1) * c_len)
            o, st = step(qc_ref[0, rows, :], zc_ref[0, rows, :], ic_ref[0, rows, :], st, rev, ctx_out)
            if ctx_out:
                if rev:
                    yc_ref[0, rows, :] = finish(o_scr[rows, :] + o, gc_ref[0, rows, :]).astype(yc_ref.dtype)
                else:
                    o_scr[rows, :] = o

        def lat(i, st, rev=rev, z_ref=z_ref):
            ci = (n_lat - 1 - i) if rev else i
            rows = pl.ds(pl.multiple_of(ci * c_len, c_len), c_len)
            srows = pl.ds(pl.multiple_of((ci + n_ctx) * c_len, c_len), c_len)
            o, st = step(q_ref[0, rows, :], z_ref[0, rows, :], i_ref[0, rows, :], st, rev, True)
            if rev:
                y_ref[0, rows, :] = finish(o_scr[srows, :] + o, g_ref[0, rows, :]).astype(y_ref.dtype)
            else:
                o_scr[srows, :] = o
            return st

        lax.fori_loop(0, n_lat, lat, st)


def _split_dot_lhs(w, x):
    hi = x.astype(BF16)
    lo = (x - hi.astype(F32)).astype(BF16)
    return _dot(w, hi) + _dot(w, lo)


def _hgrn2(p, pc, lower_bound, ctx_out):
    b, n, _ = p.shape
    nc = pc.shape[1]
    out_specs = [pl.BlockSpec((1, n, LANES), lambda i, j: (i, 0, j))]
    out_shape = [jax.ShapeDtypeStruct((b, n, GROUP_W), BF16)]
    if ctx_out:
        out_specs.append(pl.BlockSpec((1, nc, LANES), lambda i, j: (i, 0, j)))
        out_shape.append(jax.ShapeDtypeStruct((b, nc, GROUP_W), BF16))
    res = pl.pallas_call(
        functools.partial(_hg_kernel, ctx_out=ctx_out),
        grid=(b, 2),
        in_specs=[pl.BlockSpec((1, 1, LANES), lambda i, j: (j, 0, 0))]
        + [_lane_block(n, 2 * s) for s in range(5)] + [_lane_block(nc, 2 * s) for s in range(5)],
        out_specs=out_specs,
        out_shape=out_shape,
        scratch_shapes=[pltpu.VMEM((n + nc, LANES), F32)],
        compiler_params=pltpu.CompilerParams(dimension_semantics=("arbitrary", "arbitrary"),
                                             vmem_limit_bytes=VMEM_LIMIT),
        name="hgrn2",
    )(lower_bound.reshape(2, 1, LANES), p, p, p, p, p, pc, pc, pc, pc, pc)
    return res if ctx_out else (res[0], None)


FF_COLS = 1024


def _out_mlp_kernel(x_ref, yr_ref, yg_ref, yw_ref, yh_ref, mod_ref, wo_ref, wu_ref, wd_ref,
                    g1_ref, b1_ref, g2_ref, b2_ref, o_ref, h_scr):
    x = x_ref[0]
    y = None
    for s, y_ref in enumerate((yr_ref, yg_ref, yw_ref, yh_ref)):
        part = _dot(y_ref[0], wo_ref[s * GROUP_W:(s + 1) * GROUP_W, :])
        y = part if y is None else y + part
    x1 = _ln_rows(ALPHA * x + mod_ref[0, 2:3, :] * y) * g1_ref[...] + b1_ref[...]
    h_scr[...] = (_ln_rows(x1) * (1.0 + mod_ref[0, 4:5, :]) + mod_ref[0, 3:4, :]).astype(BF16)
    acc = None
    for f in range(wu_ref.shape[1] // FF_COLS):
        cols = slice(f * FF_COLS, (f + 1) * FF_COLS)
        u = jnp.maximum(_dot(h_scr[...], wu_ref[:, cols]), 0.0)
        part = _dot((u * u).astype(BF16), wd_ref[cols, :])
        acc = part if acc is None else acc + part
    o_ref[0] = _ln_rows(ALPHA * x1 + mod_ref[0, 5:6, :] * acc) * g2_ref[...] + b2_ref[...]


def _out_mlp(x, ys, mod, w_out, w_up, w_down, g1, b1, g2, b2, tm, shared_mod):
    b, n, d = x.shape
    mod_map = (lambda i, t: (0, 0, 0)) if shared_mod else (lambda i, t: (i, 0, 0))
    full = lambda a: pl.BlockSpec(a.shape, lambda i, t: (0,) * a.ndim)
    vecs = [v.reshape(1, d) for v in (g1, b1, g2, b2)]
    return pl.pallas_call(
        _out_mlp_kernel,
        grid=(b, n // tm),
        in_specs=[pl.BlockSpec((1, tm, d), lambda i, t: (i, t, 0))]
        + [pl.BlockSpec((1, tm, GROUP_W), lambda i, t: (i, t, 0))] * 4
        + [pl.BlockSpec((1, 6, d), mod_map), full(w_out), full(w_up), full(w_down)]
        + [full(v) for v in vecs],
        out_specs=pl.BlockSpec((1, tm, d), lambda i, t: (i, t, 0)),
        out_shape=jax.ShapeDtypeStruct((b, n, d), F32),
        scratch_shapes=[pltpu.VMEM((tm, d), BF16)],
        compiler_params=pltpu.CompilerParams(dimension_semantics=("arbitrary", "arbitrary"),
                                             vmem_limit_bytes=VMEM_LIMIT),
        name="out_proj_mlp",
    )(x, *ys, mod, w_out, w_up, w_down, *vecs)


def _rope_tables(n):
    rows = n // GRID_W
    row = jnp.repeat(jnp.arange(rows), GRID_W).astype(F32)
    col = (jnp.arange(rows * GRID_W) % GRID_W).astype(F32)
    inv = ROPE_THETA ** (-jnp.arange(ROPE_FREQS, dtype=F32) / ROPE_FREQS)
    zero = jnp.zeros((n, ROPE_FREQS), F32)
    cos, s1, s2 = [], [], []
    for pos in (row, col):
        ang = pos[:, None] * inv
        cos += [jnp.cos(ang), jnp.cos(ang)]
        s1 += [-jnp.sin(ang), zero]
        s2 += [zero, jnp.sin(ang)]
    return tuple(jnp.tile(jnp.concatenate(t, axis=1), (1, LANES // HEAD_DIM)) for t in (cos, s1, s2))


def kernel(x, c, ctx, c_ctx, w_ada, b_ada, w_in, ret_decay_logit, gqa_q_gain, gqa_k_gain,
           swa_sink, hgrn_lb, w_out, ln1_g, ln1_b, w_up, w_down, ln2_g, ln2_b):
    batch, n, d = x.shape
    depth = w_ada.shape[0]
    rope_tabs = _rope_tables(n)
    p_lb = jax.nn.softmax(hgrn_lb.astype(F32), axis=0)
    lower_bounds = jnp.cumsum(p_lb, axis=0) - p_lb[0]
    log_decay = jax.nn.log_sigmoid(ret_decay_logit.astype(F32))

    pad = (-(batch + 1)) % 8
    cvec = jnp.concatenate([c, c_ctx[None, :], jnp.zeros((pad, d), F32)], axis=0)
    mods = _modulation(cvec, w_ada, b_ada)
    w_in_bf, w_out_bf, w_up_bf, w_down_bf = (w.astype(BF16) for w in (w_in, w_out, w_up, w_down))

    xc = ctx
    for l in range(depth):
        ctx_out = l < depth - 1
        m_lat = mods[l, :batch].reshape(batch, 6, d)
        m_ctx = mods[l, batch:batch + 1].reshape(1, 6, d)
        p = _project(x, m_lat, w_in_bf[l], 512, False)
        pc = _project(xc, m_ctx, w_in_bf[l], xc.shape[1], True)
        y_ret, yc_ret = _retention(p[0], pc[0], log_decay[l], rope_tabs, ctx_out)
        y_glb, yc_glb = _global_gqa(p[1], pc[1], gqa_q_gain[l], gqa_k_gain[l], rope_tabs, ctx_out)
        y_win, yc_win = _window_gqa(p[2], pc[2], swa_sink[l], rope_tabs, ctx_out)
        y_hg, yc_hg = _hgrn2(p[3], pc[3], lower_bounds[l], ctx_out)
        post = (w_out_bf[l], w_up_bf[l], w_down_bf[l], ln1_g[l], ln1_b[l], ln2_g[l], ln2_b[l])
        x = _out_mlp(x, (y_ret, y_glb, y_win, y_hg), m_lat, *post, 256, False)
        if ctx_out:
            xc = _out_mlp(xc, (yc_ret, yc_glb, yc_win, yc_hg), m_ctx, *post, 256, True)
    return x
```

```python
import functools

import numpy as np
import jax
import jax.numpy as jnp
from jax import lax
from jax.experimental import pallas as pl
from jax.experimental.pallas import tpu as pltpu

F32 = jnp.float32
BF16 = jnp.bfloat16

D_MODEL = 1024
CTX_LEN = 256
GRID_W = 64
HEAD_DIM = 64
GROUP_W = 256
D_FF = 4 * D_MODEL
DEPTH = 2
ROPE_THETA = 10000.0
ROPE_FREQS = HEAD_DIM // 4
WINDOW = 128
ALPHA = (2 * DEPTH) ** 0.25
EPS = 1e-6
NEG = -1e30
TINY = 1e-30
QK_SCALE = HEAD_DIM ** -0.5

LANES = 128
CHUNK = 128
N_LEVELS = 7
VMEM_LIMIT = 56 * 1024 * 1024


def _dot(a, b):
    return jnp.dot(a, b, preferred_element_type=F32)


def _dot_nt(a, b):
    return lax.dot_general(a, b, (((1,), (1,)), ((), ())), preferred_element_type=F32)


def _dot_tn(a, b):
    return lax.dot_general(a, b, (((0,), (0,)), ((), ())), preferred_element_type=F32)


def _split_dot(x, w):
    hi = x.astype(BF16)
    lo = (x - hi.astype(F32)).astype(BF16)
    return _dot(hi, w) + _dot(lo, w)


def _silu(x):
    return x * jax.nn.sigmoid(x)


def _ln_rows(x):
    mu = jnp.mean(x, axis=-1, keepdims=True)
    xc = x - mu
    var = jnp.mean(xc * xc, axis=-1, keepdims=True)
    return xc * lax.rsqrt(var + EPS)


def _rope(x, c, s1, s2):
    return x * c + pltpu.roll(x, LANES - ROPE_FREQS, 1) * s1 + pltpu.roll(x, ROPE_FREQS, 1) * s2


def _iota(shape, dim):
    return lax.broadcasted_iota(jnp.int32, shape, dim)


def _head_block_mask():
    return (_iota((LANES, LANES), 0) // HEAD_DIM) == (_iota((LANES, LANES), 1) // HEAD_DIM)


def _stack_heads(qb):
    lo = _iota(qb.shape, 1) < HEAD_DIM
    zero = jnp.zeros_like(qb)
    return jnp.concatenate([jnp.where(lo, qb, zero), jnp.where(lo, zero, qb)], axis=0)


def _unstack_heads(r):
    t = r.shape[0] // 2
    lo = _iota((t, LANES), 1) < HEAD_DIM
    return jnp.where(lo, r[:t], r[t:])


def _mod_kernel(c_ref, w_ref, b_ref, o_ref):
    s = _silu(c_ref[...])
    o_ref[0] = _dot(s.astype(BF16), w_ref[0].astype(BF16)) + b_ref[0]


def _modulation(cvec, w_ada, b_ada):
    rows = cvec.shape[0]
    depth, d, n = w_ada.shape
    nt = n // d
    return pl.pallas_call(
        _mod_kernel,
        grid=(depth, nt),
        in_specs=[pl.BlockSpec((rows, d), lambda l, j: (0, 0)),
                  pl.BlockSpec((1, d, d), lambda l, j: (l, 0, j)),
                  pl.BlockSpec((1, 1, d), lambda l, j: (l, 0, j))],
        out_specs=pl.BlockSpec((1, rows, d), lambda l, j: (l, 0, j)),
        out_shape=jax.ShapeDtypeStruct((depth, rows, n), F32),
        compiler_params=pltpu.CompilerParams(dimension_semantics=("arbitrary", "arbitrary"),
                                             vmem_limit_bytes=VMEM_LIMIT),
        name="modulation",
    )(cvec, w_ada, b_ada.reshape(depth, 1, n))


PROJ_GROUPS = (4 * GROUP_W, 2 * GROUP_W, 2 * GROUP_W, 5 * GROUP_W)
PROJ_COLS = 256


def _proj_kernel(x_ref, mod_ref, w_ref, o_ret, o_glb, o_win, o_hg, h_scr):
    xn = _ln_rows(x_ref[0])
    shift = mod_ref[0, 0:1, :]
    scale = mod_ref[0, 1:2, :]
    h_scr[...] = (xn * (1.0 + scale) + shift).astype(BF16)
    col = 0
    for o_ref, width in zip((o_ret, o_glb, o_win, o_hg), PROJ_GROUPS):
        for j in range(width // PROJ_COLS):
            o_ref[0, :, j * PROJ_COLS:(j + 1) * PROJ_COLS] = _dot(
                h_scr[...], w_ref[:, col:col + PROJ_COLS]).astype(BF16)
            col += PROJ_COLS


def _project(x, mod, w_in_bf, tm, shared_mod):
    b, n, d = x.shape
    n_in = w_in_bf.shape[1]
    mod_map = (lambda i, t: (0, 0, 0)) if shared_mod else (lambda i, t: (i, 0, 0))
    return pl.pallas_call(
        _proj_kernel,
        grid=(b, n // tm),
        in_specs=[pl.BlockSpec((1, tm, d), lambda i, t: (i, t, 0)),
                  pl.BlockSpec((1, 6, d), mod_map),
                  pl.BlockSpec((d, n_in), lambda i, t: (0, 0))],
        out_specs=[pl.BlockSpec((1, tm, w), lambda i, t: (i, t, 0)) for w in PROJ_GROUPS],
        out_shape=[jax.ShapeDtypeStruct((b, n, w), BF16) for w in PROJ_GROUPS],
        scratch_shapes=[pltpu.VMEM((tm, d), BF16)],
        compiler_params=pltpu.CompilerParams(dimension_semantics=("arbitrary", "arbitrary"),
                                             vmem_limit_bytes=VMEM_LIMIT),
        name="adaln_in_proj",
    )(x, mod, w_in_bf)


def _ret_kernel(lg_ref, q_ref, k_ref, v_ref, g_ref, qc_ref, kc_ref, vc_ref, gc_ref,
                cos_ref, s1_ref, s2_ref, *rest, ctx_out):
    if ctx_out:
        y_ref, yc_ref, o_scr, qs_scr, ks_scr = rest
    else:
        y_ref, o_scr, qs_scr, ks_scr = rest
        yc_ref = None
    j = pl.program_id(1)
    c_len = CHUNK
    n_lat = q_ref.shape[1] // c_len
    n_ctx = qc_ref.shape[1] // c_len

    hi1 = _iota((1, LANES), 1) >= HEAD_DIM
    rowi = _iota((c_len, LANES), 0)
    coli = _iota((c_len, LANES), 1)
    pos = rowi.astype(F32)
    lgf = jnp.where(hi1, lg_ref[0, 2 * j + 1], lg_ref[0, 2 * j])
    lgb = jnp.where(hi1, lg_ref[1, 2 * j + 1], lg_ref[1, 2 * j])
    wqf = jnp.exp(lgf * (pos + 1.0))
    wqb = jnp.exp(lgb * (c_len - pos))
    wkf = jnp.exp(lgf * (c_len - 1.0 - pos))
    wkb = jnp.exp(lgb * pos)
    dcf = jnp.exp(lgf * c_len)
    dcb = jnp.exp(lgb * c_len)
    dlt = (rowi - coli).astype(F32)

    def head_mask(e):
        lf = lg_ref[0, 2 * j + e]
        lb = lg_ref[1, 2 * j + e]
        return (jnp.where(dlt >= 0, jnp.exp(lf * jnp.maximum(dlt, 0.0)), 0.0)
                + jnp.where(dlt <= 0, jnp.exp(lb * jnp.maximum(-dlt, 0.0)), 0.0))

    m2 = jnp.concatenate([head_mask(0), head_mask(1)], axis=0)
    bd = _head_block_mask()
    bdm = jnp.where(bd, 1.0 / HEAD_DIM, 0.0).astype(BF16)

    def fwd_step(c, q, k, v, st, need_out):
        k8 = k * QK_SCALE
        if need_out:
            qb = q.astype(BF16)
            a2 = _dot_nt(_stack_heads(qb), k8.astype(BF16))
            r = _dot((a2 * m2).astype(BF16), v)
            o = _unstack_heads(r) + _dot_nt((q * wqf).astype(BF16), st.astype(BF16))
            o_scr[pl.ds(c * c_len, c_len), :] = o
        qs_scr[pl.ds(c * c_len, c_len), :] = q
        ks_scr[pl.ds(c * c_len, c_len), :] = k8
        u = _dot_tn(v, (k8 * wkf).astype(BF16))
        return st * dcf + jnp.where(bd, u, 0.0)

    def bwd_step(c, v, g, st, out_ref, row0, need_out):
        q = qs_scr[pl.ds(c * c_len, c_len), :]
        k8 = ks_scr[pl.ds(c * c_len, c_len), :]
        if need_out:
            o = o_scr[pl.ds(c * c_len, c_len), :] + _dot_nt((q * wqb).astype(BF16), st.astype(BF16))
            mu = _split_dot(o, bdm)
            xc = o - mu
            var = _dot((xc * xc).astype(BF16), bdm)
            out_ref[0, pl.ds(row0, c_len), :] = (xc * lax.rsqrt(var + EPS) * _silu(g)).astype(out_ref.dtype)
        u = _dot_tn(v, (k8 * wkb).astype(BF16))
        return st * dcb + jnp.where(bd, u, 0.0)

    st = jnp.zeros((LANES, LANES), F32)
    for c in range(n_ctx):
        rows = slice(c * c_len, (c + 1) * c_len)
        st = fwd_step(c, qc_ref[0, rows, :].astype(F32), kc_ref[0, rows, :].astype(F32),
                      vc_ref[0, rows, :], st, ctx_out)

    def lat_fwd(i, st):
        rows = pl.ds(pl.multiple_of(i * c_len, c_len), c_len)
        cs, a1, a2 = cos_ref[rows, :], s1_ref[rows, :], s2_ref[rows, :]
        q = _rope(q_ref[0, rows, :].astype(F32), cs, a1, a2)
        k = _rope(k_ref[0, rows, :].astype(F32), cs, a1, a2)
        return fwd_step(i + n_ctx, q, k, v_ref[0, rows, :], st, True)

    lax.fori_loop(0, n_lat, lat_fwd, st, unroll=4)

    st = jnp.zeros((LANES, LANES), F32)
    for c in reversed(range(n_ctx)):
        rows = slice(c * c_len, (c + 1) * c_len)
        st = bwd_step(c, vc_ref[0, rows, :], gc_ref[0, rows, :].astype(F32) if ctx_out else None,
                      st, yc_ref, c * c_len, ctx_out)

    def lat_bwd(i, st):
        ci = n_lat - 1 - i
        row0 = pl.multiple_of(ci * c_len, c_len)
        rows = pl.ds(row0, c_len)
        return bwd_step(ci + n_ctx, v_ref[0, rows, :], g_ref[0, rows, :].astype(F32), st, y_ref, row0, True)

    lax.fori_loop(0, n_lat, lat_bwd, st, unroll=4)


def _lane_block(n, idx):
    return pl.BlockSpec((1, n, LANES), lambda i, j, idx=idx: (i, 0, idx + j))


def _retention(p, pc, log_decay, rope_tabs, ctx_out):
    b, n, _ = p.shape
    nc = pc.shape[1]
    tab = pl.BlockSpec((n, LANES), lambda i, j: (0, 0))
    out_specs = [pl.BlockSpec((1, n, LANES), lambda i, j: (i, 0, j))]
    out_shape = [jax.ShapeDtypeStruct((b, n, GROUP_W), BF16)]
    if ctx_out:
        out_specs.append(pl.BlockSpec((1, nc, LANES), lambda i, j: (i, 0, j)))
        out_shape.append(jax.ShapeDtypeStruct((b, nc, GROUP_W), BF16))
    res = pl.pallas_call(
        functools.partial(_ret_kernel, ctx_out=ctx_out),
        grid=(b, 2),
        in_specs=[pl.BlockSpec(memory_space=pltpu.SMEM)]
        + [_lane_block(n, 2 * s) for s in range(4)] + [_lane_block(nc, 2 * s) for s in range(4)]
        + [tab, tab, tab],
        out_specs=out_specs,
        out_shape=out_shape,
        scratch_shapes=[pltpu.VMEM((n + nc, LANES), F32)] * 3,
        compiler_params=pltpu.CompilerParams(dimension_semantics=("arbitrary", "arbitrary"),
                                             vmem_limit_bytes=VMEM_LIMIT),
        name="retention",
    )(log_decay, p, p, p, p, pc, pc, pc, pc, *rope_tabs)
    return res if ctx_out else (res[0], None)


def _own_head_dup(x, j):
    own = (_iota(x.shape, 1) // HEAD_DIM) == j
    return jnp.where(own, x, pltpu.roll(x, HEAD_DIM, 1))


def _values_with_ones(v, j):
    lo = _iota(v.shape, 1) < HEAD_DIM
    return jnp.where(lo, _own_head_dup(v, j), 1.0)


def _normalize_heads(r):
    t = r.shape[0] // 2
    top, bot = r[:t], r[t:]
    lo = _iota((t, LANES), 1) < HEAD_DIM
    return jnp.where(lo, top / pltpu.roll(top, HEAD_DIM, 1), pltpu.roll(bot, HEAD_DIM, 1) / bot)


GLB_KEYS = 768


def _rms_heads(x, bdm, gain):
    return x * lax.rsqrt(_split_dot(x * x, bdm) + EPS) * gain


def _glb_kernel(q_ref, k_ref, v_ref, qc_ref, kc_ref, vc_ref, qg_ref, kg_ref,
                cos_ref, s1_ref, s2_ref, *rest, ctx_out):
    n = q_ref.shape[1]
    nc = qc_ref.shape[1]
    if ctx_out:
        y_ref, yc_ref, kd_scr, vd_scr, q2_scr, s_scr = rest
    else:
        y_ref, kd_scr, vd_scr, q2_scr, s_scr = rest
        yc_ref = None
    j = pl.program_id(1)
    tq = CHUNK
    bdm = jnp.where(_head_block_mask(), 1.0 / HEAD_DIM, 0.0).astype(BF16)
    qg = qg_ref[...]
    kg = kg_ref[...]

    for t in range(nc // tq):
        rows = slice(t * tq, (t + 1) * tq)
        kn = _rms_heads(kc_ref[0, rows, :].astype(F32), bdm, kg)
        kd_scr[rows, :] = _own_head_dup(kn, j).astype(BF16)
        vd_scr[rows, :] = _values_with_ones(vc_ref[0, rows, :].astype(F32), j).astype(BF16)

    def prep(t, carry):
        rows = pl.ds(pl.multiple_of(t * tq, tq), tq)
        dst = pl.ds(pl.multiple_of(nc + t * tq, tq), tq)
        kn = _rms_heads(k_ref[0, rows, :].astype(F32), bdm, kg)
        kn = _rope(kn, cos_ref[rows, :], s1_ref[rows, :], s2_ref[rows, :])
        kd_scr[dst, :] = _own_head_dup(kn, j).astype(BF16)
        vd_scr[dst, :] = _values_with_ones(v_ref[0, rows, :].astype(F32), j).astype(BF16)
        return carry

    lax.fori_loop(0, n // tq, prep, 0)

    n_chunks = (n + nc) // GLB_KEYS
    n_tiles = n // tq

    def scores(q2, c):
        return _dot_nt(q2, kd_scr[c * GLB_KEYS:(c + 1) * GLB_KEYS, :])

    def stacked_q(t):
        rows = pl.ds(pl.multiple_of(t * tq, tq), tq)
        qn = _rms_heads(q_ref[0, rows, :].astype(F32), bdm, qg)
        qn = _rope(qn, cos_ref[rows, :], s1_ref[rows, :], s2_ref[rows, :])
        return _stack_heads((qn * QK_SCALE).astype(BF16))

    def online_step(s, c, m, r):
        m_c = jnp.max(s, axis=-1, keepdims=True)
        m_new = m_c if m is None else jnp.maximum(m, m_c)
        pv = _dot(jnp.exp(s - m_new).astype(BF16), vd_scr[c * GLB_KEYS:(c + 1) * GLB_KEYS, :])
        return m_new, (pv if m is None else r * jnp.exp(m - m_new) + pv)

    q2_scr[...] = stacked_q(0)
    s_scr[...] = scores(q2_scr[...], 0)

    def q_tile(t, carry):
        q2 = q2_scr[...]
        s = s_scr[...]
        m = r = None
        for c in range(n_chunks):
            if c + 1 < n_chunks:
                s_next = scores(q2, c + 1)
            else:
                q2_next = stacked_q(jnp.minimum(t + 1, n_tiles - 1))
                q2_scr[...] = q2_next
                s_next = scores(q2_next, 0)
            m, r = online_step(s, c, m, r)
            s = s_next
        s_scr[...] = s
        y_ref[0, pl.ds(pl.multiple_of(t * tq, tq), tq), :] = _normalize_heads(r).astype(y_ref.dtype)
        return carry

    lax.fori_loop(0, n_tiles, q_tile, 0)

    if ctx_out:
        for t in range(nc // tq):
            rows = slice(t * tq, (t + 1) * tq)
            qn = _rms_heads(qc_ref[0, rows, :].astype(F32), bdm, qg)
            s = _dot_nt(_stack_heads((qn * QK_SCALE).astype(BF16)), kd_scr[0:nc, :])
            e = jnp.exp(s - jnp.max(s, axis=-1, keepdims=True))
            yc_ref[0, rows, :] = _normalize_heads(_dot(e.astype(BF16), vd_scr[0:nc, :])).astype(yc_ref.dtype)


def _global_gqa(p, pc, q_gain, k_gain, rope_tabs, ctx_out):
    b, n, _ = p.shape
    nc = pc.shape[1]
    tab = pl.BlockSpec((n, LANES), lambda i, j: (0, 0))
    gain = pl.BlockSpec((1, LANES), lambda i, j: (0, 0))
    kv = lambda rows, idx: pl.BlockSpec((1, rows, LANES), lambda i, j, idx=idx: (i, 0, idx))
    out_specs = [pl.BlockSpec((1, n, LANES), lambda i, j: (i, 0, j))]
    out_shape = [jax.ShapeDtypeStruct((b, n, GROUP_W), BF16)]
    if ctx_out:
        out_specs.append(pl.BlockSpec((1, nc, LANES), lambda i, j: (i, 0, j)))
        out_shape.append(jax.ShapeDtypeStruct((b, nc, GROUP_W), BF16))
    res = pl.pallas_call(
        functools.partial(_glb_kernel, ctx_out=ctx_out),
        grid=(b, 2),
        in_specs=[_lane_block(n, 0), kv(n, 2), kv(n, 3), _lane_block(nc, 0), kv(nc, 2), kv(nc, 3),
                  gain, gain, tab, tab, tab],
        out_specs=out_specs,
        out_shape=out_shape,
        scratch_shapes=[pltpu.VMEM((n + nc, LANES), BF16)] * 2
        + [pltpu.VMEM((2 * CHUNK, LANES), BF16), pltpu.VMEM((2 * CHUNK, GLB_KEYS), F32)],
        compiler_params=pltpu.CompilerParams(dimension_semantics=("arbitrary", "arbitrary"),
                                             vmem_limit_bytes=VMEM_LIMIT),
        name="global_gqa",
    )(p, p, p, pc, pc, pc, jnp.tile(q_gain, 2)[None, :], jnp.tile(k_gain, 2)[None, :], *rope_tabs)
    return res if ctx_out else (res[0], None)


def _win_kernel(sink_ref, q_ref, k_ref, v_ref, qc_ref, kc_ref, vc_ref,
                cos_ref, s1_ref, s2_ref, *rest, ctx_out):
    if ctx_out:
        y_ref, yc_ref, kd_scr, vd_scr = rest
    else:
        y_ref, kd_scr, vd_scr = rest
        yc_ref = None
    j = pl.program_id(1)
    n = q_ref.shape[1]
    nc = qc_ref.shape[1]
    tq = CHUNK
    nb = n // tq
    lat0 = nc + tq

    zeros = jnp.zeros((tq, LANES), BF16)
    for r0 in (nc, lat0 + n):
        kd_scr[r0:r0 + tq, :] = zeros
        vd_scr[r0:r0 + tq, :] = zeros
    for t in range(nc // tq):
        rows = slice(t * tq, (t + 1) * tq)
        kd_scr[rows, :] = _own_head_dup(kc_ref[0, rows, :].astype(F32), j).astype(BF16)
        vd_scr[rows, :] = _values_with_ones(vc_ref[0, rows, :].astype(F32), j).astype(BF16)

    def prep(t, carry):
        rows = pl.ds(pl.multiple_of(t * tq, tq), tq)
        dst = pl.ds(pl.multiple_of(lat0 + t * tq, tq), tq)
        kn = _rope(k_ref[0, rows, :].astype(F32), cos_ref[rows, :], s1_ref[rows, :], s2_ref[rows, :])
        kd_scr[dst, :] = _own_head_dup(kn, j).astype(BF16)
        vd_scr[dst, :] = _values_with_ones(v_ref[0, rows, :].astype(F32), j).astype(BF16)
        return carry

    lax.fori_loop(0, nb, prep, 0)

    sk = jnp.where(_iota((2 * tq, 1), 0) < tq, sink_ref[2 * j], sink_ref[2 * j + 1])

    sum_lanes = _iota((2 * tq, LANES), 1) >= HEAD_DIM

    def softmax_pv(parts, sk):
        m = sk
        for s, _ in parts:
            m = jnp.maximum(m, jnp.max(s, axis=-1, keepdims=True))
        acc = jnp.where(sum_lanes, jnp.exp(sk - m), 0.0)
        for s, vals in parts:
            acc = acc + _dot(jnp.exp(s - m).astype(BF16), vals)
        return _normalize_heads(acc)

    band_c = _iota((2 * tq, 3 * tq), 1)
    off = band_c - _iota((2 * tq, 3 * tq), 0) % tq
    band_bias = jnp.where((off >= 0) & (off <= 2 * WINDOW), 0.0, NEG)
    col1 = _iota((1, 3 * tq), 1)
    first_bias = jnp.where(col1 < tq, NEG, 0.0)
    last_bias = jnp.where(col1 >= 2 * tq, NEG, 0.0)

    def q_tile(t, carry):
        rows = pl.ds(pl.multiple_of(t * tq, tq), tq)
        band = pl.ds(pl.multiple_of(nc + t * tq, tq), 3 * tq)
        qn = _rope(q_ref[0, rows, :].astype(F32), cos_ref[rows, :], s1_ref[rows, :], s2_ref[rows, :])
        q2 = _stack_heads((qn * QK_SCALE).astype(BF16))
        s_ctx = _dot_nt(q2, kd_scr[0:nc, :])
        edge = jnp.where(t == 0, first_bias, 0.0) + jnp.where(t == nb - 1, last_bias, 0.0)
        s_band = _dot_nt(q2, kd_scr[band, :]) + (band_bias + edge)
        y_ref[0, rows, :] = softmax_pv([(s_ctx, vd_scr[0:nc, :]), (s_band, vd_scr[band, :])], sk).astype(y_ref.dtype)
        return carry

    lax.fori_loop(0, nb, q_tile, 0, unroll=2)

    if ctx_out:
        for t in range(nc // tq):
            rows = slice(t * tq, (t + 1) * tq)
            q2 = _stack_heads((qc_ref[0, rows, :].astype(F32) * QK_SCALE).astype(BF16))
            s_ctx = _dot_nt(q2, kd_scr[0:nc, :])
            yc_ref[0, rows, :] = softmax_pv([(s_ctx, vd_scr[0:nc, :])], sk).astype(yc_ref.dtype)


def _window_gqa(p, pc, sink, rope_tabs, ctx_out):
    b, n, _ = p.shape
    nc = pc.shape[1]
    tab = pl.BlockSpec((n, LANES), lambda i, j: (0, 0))
    kv = lambda rows, idx: pl.BlockSpec((1, rows, LANES), lambda i, j, idx=idx: (i, 0, idx))
    out_specs = [pl.BlockSpec((1, n, LANES), lambda i, j: (i, 0, j))]
    out_shape = [jax.ShapeDtypeStruct((b, n, GROUP_W), BF16)]
    if ctx_out:
        out_specs.append(pl.BlockSpec((1, nc, LANES), lambda i, j: (i, 0, j)))
        out_shape.append(jax.ShapeDtypeStruct((b, nc, GROUP_W), BF16))
    res = pl.pallas_call(
        functools.partial(_win_kernel, ctx_out=ctx_out),
        grid=(b, 2),
        in_specs=[pl.BlockSpec(memory_space=pltpu.SMEM),
                  _lane_block(n, 0), kv(n, 2), kv(n, 3), _lane_block(nc, 0), kv(nc, 2), kv(nc, 3),
                  tab, tab, tab],
        out_specs=out_specs,
        out_shape=out_shape,
        scratch_shapes=[pltpu.VMEM((n + nc + 2 * CHUNK, LANES), BF16)] * 2,
        compiler_params=pltpu.CompilerParams(dimension_semantics=("arbitrary", "arbitrary"),
                                             vmem_limit_bytes=VMEM_LIMIT),
        name="window_gqa",
    )(sink, p, p, p, pc, pc, pc, *rope_tabs)
    return res if ctx_out else (res[0], None)


def _mid_rows(b, level, rev):
    c_len = b.shape[0]
    h = 1 << level
    mid_off = h if rev else h - 1
    if 2 * h >= 16:
        pieces = [jnp.broadcast_to(b[m * 2 * h + mid_off:m * 2 * h + mid_off + 1, :], (2 * h, b.shape[1]))
                  for m in range(c_len // (2 * h))]
        return pieces[0] if len(pieces) == 1 else jnp.concatenate(pieces, axis=0)
    r = _iota(b.shape, 0) % (2 * h)
    out = b
    for rr in range(2 * h):
        if rr != mid_off:
            out = jnp.where(r == rr, pltpu.roll(b, (rr - mid_off) % c_len, 0), out)
    return out


def _level_masks():
    t = np.arange(2 * CHUNK)[:, None] % CHUNK
    s = np.arange(CHUNK)[None, :]
    out = np.zeros((2, N_LEVELS, 2 * CHUNK, CHUNK), np.float32)
    for level in range(N_LEVELS):
        same = (t >> (level + 1)) == (s >> (level + 1))
        t_hi, s_hi = (t >> level) & 1, (s >> level) & 1
        out[0, level] = same & (t_hi == 1) & (s_hi == 0)
        out[1, level] = same & (t_hi == 0) & (s_hi == 1)
    return out


def _hg_kernel(lvl_ref, lb_ref, q_ref, zf_ref, zb_ref, i_ref, g_ref, qc_ref, zfc_ref, zbc_ref, ic_ref, gc_ref,
               *rest, ctx_out):
    if ctx_out:
        y_ref, yc_ref, o_scr = rest
    else:
        y_ref, o_scr = rest
        yc_ref = None
    c_len = CHUNK
    n_lat = q_ref.shape[1] // c_len
    n_ctx = qc_ref.shape[1] // c_len
    lb = lb_ref[0]
    bd = _head_block_mask()
    bd_ones = jnp.where(bd, 1.0, 0.0).astype(BF16)
    bdm = jnp.where(bd, 1.0 / HEAD_DIM, 0.0).astype(BF16)
    rowi = _iota((c_len, c_len), 0)
    coli = _iota((c_len, c_len), 1)
    tri = (jnp.where(coli <= rowi, 1.0, 0.0).astype(BF16), jnp.where(coli >= rowi, 1.0, 0.0).astype(BF16))

    def step(qraw, z, v, st, rev, need_out):
        sg = jax.nn.sigmoid(z.astype(F32))
        f = lb + (1.0 - lb) * sg
        x = jnp.log(jnp.maximum(f, TINY))
        kk = (1.0 - lb) * (1.0 - sg)
        b = _split_dot_lhs(tri[1 if rev else 0], x)
        edge = b[0:1, :] if rev else b[c_len - 1:c_len, :]
        o = None
        if need_out:
            qs = _silu(qraw.astype(F32))
            o = _dot_nt((qs * jnp.exp(b)).astype(BF16), st.astype(BF16))
            o = o + _split_dot(qs * kk, bd_ones) * v.astype(F32)
            a = None
            for level in range(N_LEVELS):
                e = jnp.exp(-jnp.abs(b - _mid_rows(b, level, rev)))
                al = _dot_nt(_stack_heads((qs * e).astype(BF16)), (kk * e).astype(BF16))
                al = al * lvl_ref[1 if rev else 0, level]
                a = al if a is None else a + al
            o = o + _unstack_heads(_dot(a.astype(BF16), v))
        u = _dot_tn(v, (kk * jnp.exp(edge - b)).astype(BF16))
        return o, st * jnp.exp(edge) + jnp.where(bd, u, 0.0)

    def finish(o, g):
        return o * lax.rsqrt(_dot((o * o).astype(BF16), bdm) + EPS) * _silu(g.astype(F32))

    for rev in (False, True):
        z_ref, zc_ref = (zb_ref, zbc_ref) if rev else (zf_ref, zfc_ref)
        st = jnp.zeros((LANES, LANES), F32)
        for c in (reversed(range(n_ctx)) if rev else range(n_ctx)):
            rows = slice(c * c_len, (c + 1) * c_len)
            o, st = step(qc_ref[0, rows, :], zc_ref[0, rows, :], ic_ref[0, rows, :], st, rev, ctx_out)
            if ctx_out:
                if rev:
                    yc_ref[0, rows, :] = finish(o_scr[rows, :] + o, gc_ref[0, rows, :]).astype(yc_ref.dtype)
                else:
                    o_scr[rows, :] = o

        def lat(i, st, rev=rev, z_ref=z_ref):
            ci = (n_lat - 1 - i) if rev else i
            rows = pl.ds(pl.multiple_of(ci * c_len, c_len), c_len)
            srows = pl.ds(pl.multiple_of((ci + n_ctx) * c_len, c_len), c_len)
            o, st = step(q_ref[0, rows, :], z_ref[0, rows, :], i_ref[0, rows, :], st, rev, True)
            if rev:
                y_ref[0, rows, :] = finish(o_scr[srows, :] + o, g_ref[0, rows, :]).astype(y_ref.dtype)
            else:
                o_scr[srows, :] = o
            return st

        lax.fori_loop(0, n_lat, lat, st, unroll=2)


def _split_dot_lhs(w, x):
    hi = x.astype(BF16)
    lo = (x - hi.astype(F32)).astype(BF16)
    return _dot(w, hi) + _dot(w, lo)


def _hgrn2(p, pc, lower_bound, ctx_out):
    b, n, _ = p.shape
    nc = pc.shape[1]
    out_specs = [pl.BlockSpec((1, n, LANES), lambda i, j: (i, 0, j))]
    out_shape = [jax.ShapeDtypeStruct((b, n, GROUP_W), BF16)]
    if ctx_out:
        out_specs.append(pl.BlockSpec((1, nc, LANES), lambda i, j: (i, 0, j)))
        out_shape.append(jax.ShapeDtypeStruct((b, nc, GROUP_W), BF16))
    res = pl.pallas_call(
        functools.partial(_hg_kernel, ctx_out=ctx_out),
        grid=(b, 2),
        in_specs=[pl.BlockSpec((2, N_LEVELS, 2 * CHUNK, CHUNK), lambda i, j: (0, 0, 0, 0)),
                  pl.BlockSpec((1, 1, LANES), lambda i, j: (j, 0, 0))]
        + [_lane_block(n, 2 * s) for s in range(5)] + [_lane_block(nc, 2 * s) for s in range(5)],
        out_specs=out_specs,
        out_shape=out_shape,
        scratch_shapes=[pltpu.VMEM((n + nc, LANES), F32)],
        compiler_params=pltpu.CompilerParams(dimension_semantics=("arbitrary", "arbitrary"),
                                             vmem_limit_bytes=VMEM_LIMIT),
        name="hgrn2",
    )(jnp.asarray(_level_masks()), lower_bound.reshape(2, 1, LANES), p, p, p, p, p, pc, pc, pc, pc, pc)
    return res if ctx_out else (res[0], None)


FF_COLS = 1024


def _out_mlp_kernel(x_ref, yr_ref, yg_ref, yw_ref, yh_ref, mod_ref, wo_ref, wu_ref, wd_ref,
                    g1_ref, b1_ref, g2_ref, b2_ref, o_ref, h_scr):
    x = x_ref[0]
    y = None
    for s, y_ref in enumerate((yr_ref, yg_ref, yw_ref, yh_ref)):
        part = _dot(y_ref[0], wo_ref[s * GROUP_W:(s + 1) * GROUP_W, :])
        y = part if y is None else y + part
    x1 = _ln_rows(ALPHA * x + mod_ref[0, 2:3, :] * y) * g1_ref[...] + b1_ref[...]
    h_scr[...] = (_ln_rows(x1) * (1.0 + mod_ref[0, 4:5, :]) + mod_ref[0, 3:4, :]).astype(BF16)
    acc = None
    for f in range(wu_ref.shape[1] // FF_COLS):
        cols = slice(f * FF_COLS, (f + 1) * FF_COLS)
        u = jnp.maximum(_dot(h_scr[...], wu_ref[:, cols]), 0.0)
        part = _dot((u * u).astype(BF16), wd_ref[cols, :])
        acc = part if acc is None else acc + part
    o_ref[0] = _ln_rows(ALPHA * x1 + mod_ref[0, 5:6, :] * acc) * g2_ref[...] + b2_ref[...]


def _out_mlp(x, ys, mod, w_out, w_up, w_down, g1, b1, g2, b2, tm, shared_mod):
    b, n, d = x.shape
    mod_map = (lambda i, t: (0, 0, 0)) if shared_mod else (lambda i, t: (i, 0, 0))
    full = lambda a: pl.BlockSpec(a.shape, lambda i, t: (0,) * a.ndim)
    vecs = [v.reshape(1, d) for v in (g1, b1, g2, b2)]
    return pl.pallas_call(
        _out_mlp_kernel,
        grid=(b, n // tm),
        in_specs=[pl.BlockSpec((1, tm, d), lambda i, t: (i, t, 0))]
        + [pl.BlockSpec((1, tm, GROUP_W), lambda i, t: (i, t, 0))] * 4
        + [pl.BlockSpec((1, 6, d), mod_map), full(w_out), full(w_up), full(w_down)]
        + [full(v) for v in vecs],
        out_specs=pl.BlockSpec((1, tm, d), lambda i, t: (i, t, 0)),
        out_shape=jax.ShapeDtypeStruct((b, n, d), F32),
        scratch_shapes=[pltpu.VMEM((tm, d), BF16)],
        compiler_params=pltpu.CompilerParams(dimension_semantics=("arbitrary", "arbitrary"),
                                             vmem_limit_bytes=VMEM_LIMIT),
        name="out_proj_mlp",
    )(x, *ys, mod, w_out, w_up, w_down, *vecs)


def _rope_tables(n):
    rows = n // GRID_W
    row = jnp.repeat(jnp.arange(rows), GRID_W).astype(F32)
    col = (jnp.arange(rows * GRID_W) % GRID_W).astype(F32)
    inv = ROPE_THETA ** (-jnp.arange(ROPE_FREQS, dtype=F32) / ROPE_FREQS)
    zero = jnp.zeros((n, ROPE_FREQS), F32)
    cos, s1, s2 = [], [], []
    for pos in (row, col):
        ang = pos[:, None] * inv
        cos += [jnp.cos(ang), jnp.cos(ang)]
        s1 += [-jnp.sin(ang), zero]
        s2 += [zero, jnp.sin(ang)]
    return tuple(jnp.tile(jnp.concatenate(t, axis=1), (1, LANES // HEAD_DIM)) for t in (cos, s1, s2))


def kernel(x, c, ctx, c_ctx, w_ada, b_ada, w_in, ret_decay_logit, gqa_q_gain, gqa_k_gain,
           swa_sink, hgrn_lb, w_out, ln1_g, ln1_b, w_up, w_down, ln2_g, ln2_b):
    batch, n, d = x.shape
    depth = w_ada.shape[0]
    rope_tabs = _rope_tables(n)
    p_lb = jax.nn.softmax(hgrn_lb.astype(F32), axis=0)
    lower_bounds = jnp.cumsum(p_lb, axis=0) - p_lb[0]
    log_decay = jax.nn.log_sigmoid(ret_decay_logit.astype(F32))

    pad = (-(batch + 1)) % 8
    cvec = jnp.concatenate([c, c_ctx[None, :], jnp.zeros((pad, d), F32)], axis=0)
    mods = _modulation(cvec, w_ada, b_ada)
    w_in_bf, w_out_bf, w_up_bf, w_down_bf = (w.astype(BF16) for w in (w_in, w_out, w_up, w_down))

    xc = ctx
    for l in range(depth):
        ctx_out = l < depth - 1
        m_lat = mods[l, :batch].reshape(batch, 6, d)
        m_ctx = mods[l, batch:batch + 1].reshape(1, 6, d)
        p = _project(x, m_lat, w_in_bf[l], 512, False)
        pc = _project(xc, m_ctx, w_in_bf[l], xc.shape[1], True)
        y_ret, yc_ret = _retention(p[0], pc[0], log_decay[l], rope_tabs, ctx_out)
        y_glb, yc_glb = _global_gqa(p[1], pc[1], gqa_q_gain[l], gqa_k_gain[l], rope_tabs, ctx_out)
        y_win, yc_win = _window_gqa(p[2], pc[2], swa_sink[l], rope_tabs, ctx_out)
        y_hg, yc_hg = _hgrn2(p[3], pc[3], lower_bounds[l], ctx_out)
        post = (w_out_bf[l], w_up_bf[l], w_down_bf[l], ln1_g[l], ln1_b[l], ln2_g[l], ln2_b[l])
        x = _out_mlp(x, (y_ret, y_glb, y_win, y_hg), m_lat, *post, 256, False)
        if ctx_out:
            xc = _out_mlp(xc, (yc_ret, yc_glb, yc_win, yc_hg), m_ctx, *post, 256, True)
    return x
```

```python
import functools

import numpy as np
import jax
import jax.numpy as jnp
from jax import lax
from jax.experimental import pallas as pl
from jax.experimental.pallas import tpu as pltpu

F32 = jnp.float32
BF16 = jnp.bfloat16

D_MODEL = 1024
CTX_LEN = 256
GRID_W = 64
HEAD_DIM = 64
GROUP_W = 256
D_FF = 4 * D_MODEL
DEPTH = 2
ROPE_THETA = 10000.0
ROPE_FREQS = HEAD_DIM // 4
WINDOW = 128
ALPHA = (2 * DEPTH) ** 0.25
EPS = 1e-6
NEG = -1e30
TINY = 1e-30
QK_SCALE = HEAD_DIM ** -0.5

LANES = 128
CHUNK = 128
N_LEVELS = 7
VMEM_LIMIT = 56 * 1024 * 1024


def _dot(a, b):
    return jnp.dot(a, b, preferred_element_type=F32)


def _dot_nt(a, b):
    return lax.dot_general(a, b, (((1,), (1,)), ((), ())), preferred_element_type=F32)


def _dot_tn(a, b):
    return lax.dot_general(a, b, (((0,), (0,)), ((), ())), preferred_element_type=F32)


def _split_dot(x, w):
    hi = x.astype(BF16)
    lo = (x - hi.astype(F32)).astype(BF16)
    return _dot(hi, w) + _dot(lo, w)


def _silu(x):
    return x * jax.nn.sigmoid(x)


def _ln_rows(x):
    mu = jnp.mean(x, axis=-1, keepdims=True)
    xc = x - mu
    var = jnp.mean(xc * xc, axis=-1, keepdims=True)
    return xc * lax.rsqrt(var + EPS)


def _rope_roll(x, c, sn):
    first = (_iota(x.shape, 1) & ROPE_FREQS) == 0
    partner = jnp.where(first, pltpu.roll(x, LANES - ROPE_FREQS, 1), pltpu.roll(x, ROPE_FREQS, 1))
    return x * c + partner * sn


def _iota(shape, dim):
    return lax.broadcasted_iota(jnp.int32, shape, dim)


def _head_block_mask():
    return (_iota((LANES, LANES), 0) // HEAD_DIM) == (_iota((LANES, LANES), 1) // HEAD_DIM)


def _stack_heads(qb):
    lo = _iota(qb.shape, 1) < HEAD_DIM
    zero = jnp.zeros_like(qb)
    return jnp.concatenate([jnp.where(lo, qb, zero), jnp.where(lo, zero, qb)], axis=0)


def _unstack_heads(r):
    t = r.shape[0] // 2
    lo = _iota((t, LANES), 1) < HEAD_DIM
    return jnp.where(lo, r[:t], r[t:])


def _mod_kernel(c_ref, w_ref, b_ref, o_ref):
    s = _silu(c_ref[...])
    o_ref[0] = _dot(s.astype(BF16), w_ref[0].astype(BF16)) + b_ref[0]


def _modulation(cvec, w_ada, b_ada):
    rows = cvec.shape[0]
    depth, d, n = w_ada.shape
    nt = n // d
    return pl.pallas_call(
        _mod_kernel,
        grid=(depth, nt),
        in_specs=[pl.BlockSpec((rows, d), lambda l, j: (0, 0)),
                  pl.BlockSpec((1, d, d), lambda l, j: (l, 0, j)),
                  pl.BlockSpec((1, 1, d), lambda l, j: (l, 0, j))],
        out_specs=pl.BlockSpec((1, rows, d), lambda l, j: (l, 0, j)),
        out_shape=jax.ShapeDtypeStruct((depth, rows, n), F32),
        compiler_params=pltpu.CompilerParams(dimension_semantics=("arbitrary", "arbitrary"),
                                             vmem_limit_bytes=VMEM_LIMIT),
        name="modulation",
    )(cvec, w_ada, b_ada.reshape(depth, 1, n))


PROJ_GROUPS = (4 * GROUP_W, 2 * GROUP_W, 2 * GROUP_W, 5 * GROUP_W)
PROJ_COLS = 256


def _proj_kernel(x_ref, mod_ref, w_ref, o_ret, o_glb, o_win, o_hg, h_scr):
    half = x_ref.shape[1] // 2
    halves = (slice(0, half), slice(half, 2 * half))

    def modulate(r):
        h_scr[r, :] = (_ln_rows(x_ref[0, r, :]) * (1.0 + mod_ref[0, 1:2, :]) + mod_ref[0, 0:1, :]).astype(BF16)

    def project(r, first_chunk_only):
        col = 0
        for o_ref, width in zip((o_ret, o_glb, o_win, o_hg), PROJ_GROUPS):
            for j in range(width // PROJ_COLS):
                if first_chunk_only == (col == 0):
                    o_ref[0, r, j * PROJ_COLS:(j + 1) * PROJ_COLS] = _dot(
                        h_scr[r, :], w_ref[:, col:col + PROJ_COLS]).astype(BF16)
                col += PROJ_COLS

    modulate(halves[0])
    project(halves[0], True)
    modulate(halves[1])
    project(halves[0], False)
    project(halves[1], True)
    project(halves[1], False)


def _project(x, mod, w_in_bf, tm, shared_mod):
    b, n, d = x.shape
    n_in = w_in_bf.shape[1]
    mod_map = (lambda i, t: (0, 0, 0)) if shared_mod else (lambda i, t: (i, 0, 0))
    return pl.pallas_call(
        _proj_kernel,
        grid=(b, n // tm),
        in_specs=[pl.BlockSpec((1, tm, d), lambda i, t: (i, t, 0)),
                  pl.BlockSpec((1, 6, d), mod_map),
                  pl.BlockSpec((d, n_in), lambda i, t: (0, 0), pipeline_mode=pl.Buffered(1))],
        out_specs=[pl.BlockSpec((1, tm, w), lambda i, t: (i, t, 0)) for w in PROJ_GROUPS],
        out_shape=[jax.ShapeDtypeStruct((b, n, w), BF16) for w in PROJ_GROUPS],
        scratch_shapes=[pltpu.VMEM((tm, d), BF16)],
        compiler_params=pltpu.CompilerParams(dimension_semantics=("arbitrary", "arbitrary"),
                                             vmem_limit_bytes=VMEM_LIMIT),
        name="adaln_in_proj",
    )(x, mod, w_in_bf)


def _ret_kernel(lg_ref, q_ref, k_ref, v_ref, g_ref, qc_ref, kc_ref, vc_ref, gc_ref,
                cos_ref, sin_ref, *rest, ctx_out):
    if ctx_out:
        y_ref, yc_ref, o_scr, qs_scr, ks_scr = rest
    else:
        y_ref, o_scr, qs_scr, ks_scr = rest
        yc_ref = None
    j = pl.program_id(1)
    c_len = CHUNK
    n_lat = q_ref.shape[1] // c_len
    n_ctx = qc_ref.shape[1] // c_len

    hi1 = _iota((1, LANES), 1) >= HEAD_DIM
    rowi = _iota((c_len, LANES), 0)
    coli = _iota((c_len, LANES), 1)
    pos = rowi.astype(F32)
    lgf = jnp.where(hi1, lg_ref[0, 2 * j + 1], lg_ref[0, 2 * j])
    lgb = jnp.where(hi1, lg_ref[1, 2 * j + 1], lg_ref[1, 2 * j])
    wqf = jnp.exp(lgf * (pos + 1.0))
    wqb = jnp.exp(lgb * (c_len - pos))
    wkf = jnp.exp(lgf * (c_len - 1.0 - pos))
    wkb = jnp.exp(lgb * pos)
    dcf = jnp.exp(lgf * c_len)
    dcb = jnp.exp(lgb * c_len)
    dlt = (rowi - coli).astype(F32)

    def head_mask(e):
        lf = lg_ref[0, 2 * j + e]
        lb = lg_ref[1, 2 * j + e]
        return (jnp.where(dlt >= 0, jnp.exp(lf * jnp.maximum(dlt, 0.0)), 0.0)
                + jnp.where(dlt <= 0, jnp.exp(lb * jnp.maximum(-dlt, 0.0)), 0.0))

    m2 = jnp.concatenate([head_mask(0), head_mask(1)], axis=0)
    bd = _head_block_mask()
    bdm = jnp.where(bd, 1.0 / HEAD_DIM, 0.0).astype(BF16)

    def fwd_step(c, q, k, v, st, need_out):
        k8 = k * QK_SCALE
        if need_out:
            qb = q.astype(BF16)
            a2 = _dot_nt(_stack_heads(qb), k8.astype(BF16))
            r = _dot((a2 * m2).astype(BF16), v)
            o = _unstack_heads(r) + _dot_nt((q * wqf).astype(BF16), st.astype(BF16))
            o_scr[pl.ds(c * c_len, c_len), :] = o
        qs_scr[pl.ds(c * c_len, c_len), :] = q
        ks_scr[pl.ds(c * c_len, c_len), :] = k8
        u = _dot_tn(v, (k8 * wkf).astype(BF16))
        return st * dcf + jnp.where(bd, u, 0.0)

    def bwd_step(c, v, g, st, out_ref, row0, need_out):
        q = qs_scr[pl.ds(c * c_len, c_len), :]
        k8 = ks_scr[pl.ds(c * c_len, c_len), :]
        if need_out:
            o = o_scr[pl.ds(c * c_len, c_len), :] + _dot_nt((q * wqb).astype(BF16), st.astype(BF16))
            mu = _split_dot(o, bdm)
            xc = o - mu
            var = _dot((xc * xc).astype(BF16), bdm)
            out_ref[0, pl.ds(row0, c_len), :] = (xc * lax.rsqrt(var + EPS) * _silu(g)).astype(out_ref.dtype)
        u = _dot_tn(v, (k8 * wkb).astype(BF16))
        return st * dcb + jnp.where(bd, u, 0.0)

    st = jnp.zeros((LANES, LANES), F32)
    for c in range(n_ctx):
        rows = slice(c * c_len, (c + 1) * c_len)
        st = fwd_step(c, qc_ref[0, rows, :].astype(F32), kc_ref[0, rows, :].astype(F32),
                      vc_ref[0, rows, :], st, ctx_out)

    def lat_fwd(i, st):
        rows = pl.ds(pl.multiple_of(i * c_len, c_len), c_len)
        cs, sn = cos_ref[rows, :], sin_ref[rows, :]
        q = _rope_roll(q_ref[0, rows, :].astype(F32), cs, sn)
        k = _rope_roll(k_ref[0, rows, :].astype(F32), cs, sn)
        return fwd_step(i + n_ctx, q, k, v_ref[0, rows, :], st, True)

    lax.fori_loop(0, n_lat, lat_fwd, st, unroll=4)

    st = jnp.zeros((LANES, LANES), F32)
    for c in reversed(range(n_ctx)):
        rows = slice(c * c_len, (c + 1) * c_len)
        st = bwd_step(c, vc_ref[0, rows, :], gc_ref[0, rows, :].astype(F32) if ctx_out else None,
                      st, yc_ref, c * c_len, ctx_out)

    def lat_bwd(i, st):
        ci = n_lat - 1 - i
        row0 = pl.multiple_of(ci * c_len, c_len)
        rows = pl.ds(row0, c_len)
        return bwd_step(ci + n_ctx, v_ref[0, rows, :], g_ref[0, rows, :].astype(F32), st, y_ref, row0, True)

    lax.fori_loop(0, n_lat, lat_bwd, st, unroll=4)


def _lane_block(n, idx):
    return pl.BlockSpec((1, n, LANES), lambda i, j, idx=idx: (i, 0, idx + j))


def _retention(p, pc, log_decay, rope_tabs, ctx_out):
    b, n, _ = p.shape
    nc = pc.shape[1]
    tab = pl.BlockSpec((n, LANES), lambda i, j: (0, 0))
    out_specs = [pl.BlockSpec((1, n, LANES), lambda i, j: (i, 0, j))]
    out_shape = [jax.ShapeDtypeStruct((b, n, GROUP_W), BF16)]
    if ctx_out:
        out_specs.append(pl.BlockSpec((1, nc, LANES), lambda i, j: (i, 0, j)))
        out_shape.append(jax.ShapeDtypeStruct((b, nc, GROUP_W), BF16))
    res = pl.pallas_call(
        functools.partial(_ret_kernel, ctx_out=ctx_out),
        grid=(b, 2),
        in_specs=[pl.BlockSpec(memory_space=pltpu.SMEM)]
        + [_lane_block(n, 2 * s) for s in range(4)] + [_lane_block(nc, 2 * s) for s in range(4)]
        + [tab, tab],
        out_specs=out_specs,
        out_shape=out_shape,
        scratch_shapes=[pltpu.VMEM((n + nc, LANES), F32)] * 3,
        compiler_params=pltpu.CompilerParams(dimension_semantics=("arbitrary", "arbitrary"),
                                             vmem_limit_bytes=VMEM_LIMIT),
        name="retention",
    )(log_decay, p, p, p, p, pc, pc, pc, pc, *rope_tabs)
    return res if ctx_out else (res[0], None)


def _own_head_dup(x, j):
    own = (_iota(x.shape, 1) // HEAD_DIM) == j
    return jnp.where(own, x, pltpu.roll(x, HEAD_DIM, 1))


def _values_with_ones(v, j):
    lo = _iota(v.shape, 1) < HEAD_DIM
    return jnp.where(lo, _own_head_dup(v, j), 1.0)


def _normalize_heads(r):
    t = r.shape[0] // 2
    top, bot = r[:t], r[t:]
    lo = _iota((t, LANES), 1) < HEAD_DIM
    return jnp.where(lo, top / pltpu.roll(top, HEAD_DIM, 1), pltpu.roll(bot, HEAD_DIM, 1) / bot)


GLB_KEYS = 768
GLB_TILE = 256
GLB_BOUND_SLACK = 1.01
GLB_MAX_SHIFT = 40.0


def _rms_heads(x, bdm, gain):
    return x * lax.rsqrt(_split_dot(x * x, bdm) + EPS) * gain


def _glb_kernel(shift_ref, q_ref, k_ref, v_ref, qc_ref, kc_ref, vc_ref, qg_ref, kg_ref,
                qcos_ref, qsin_ref, kcos_ref, ksin_ref, *rest, ctx_out, bounded):
    n = q_ref.shape[1]
    nc = qc_ref.shape[1]
    if ctx_out:
        y_ref, yc_ref, kd_scr, vd_scr = rest[:4]
    else:
        y_ref, kd_scr, vd_scr = rest[:3]
        yc_ref = None
    if bounded:
        qa_scr, = rest[-1:]
    else:
        q2_scr, s_scr = rest[-2:]
    j = pl.program_id(1)
    tq = CHUNK
    bdm = jnp.where(_head_block_mask(), 1.0 / HEAD_DIM, 0.0).astype(BF16)
    qg = qg_ref[...]
    kg = kg_ref[...]
    def normed_rope(xb, cos_tab, sin_tab):
        x = xb.astype(F32)
        return lax.rsqrt(_split_dot(x * x, bdm) + EPS) * _rope_roll(x, cos_tab, sin_tab)

    for t in range(nc // tq):
        rows = slice(t * tq, (t + 1) * tq)
        kn = _rms_heads(kc_ref[0, rows, :].astype(F32), bdm, kg)
        kd_scr[rows, :] = _own_head_dup(kn, j).astype(BF16)
        vd_scr[rows, :] = _values_with_ones(vc_ref[0, rows, :].astype(F32), j).astype(BF16)

    def prep(t, carry):
        rows = pl.ds(pl.multiple_of(t * tq, tq), tq)
        dst = pl.ds(pl.multiple_of(nc + t * tq, tq), tq)
        x = k_ref[0, rows, :].astype(F32)
        kn = lax.rsqrt(_split_dot(x * x, bdm) + EPS) * _rope_roll(x, kcos_ref[rows, :], ksin_ref[rows, :])
        kd_scr[dst, :] = _own_head_dup(kn, j).astype(BF16)
        vd_scr[dst, :] = _values_with_ones(v_ref[0, rows, :].astype(F32), j).astype(BF16)
        return carry

    lax.fori_loop(0, n // tq, prep, 0, unroll=4)

    n_tiles = n // tq

    if bounded:
        shift = shift_ref[0]

        def stack_queries(t, carry):
            rows = pl.ds(pl.multiple_of(t * tq, tq), tq)
            qn = normed_rope(q_ref[0, rows, :], qcos_ref[rows, :], qsin_ref[rows, :])
            qa_scr[t] = _stack_heads((qn * QK_SCALE).astype(BF16))
            return carry

        lax.fori_loop(0, n_tiles, stack_queries, 0, unroll=4)

        def bounded_tile(t, carry):
            rows = pl.ds(pl.multiple_of(t * tq, tq), tq)
            q2 = qa_scr[t]
            acc = None
            s = _dot_nt(q2, kd_scr[0:GLB_TILE, :])
            for k0 in range(0, n + nc, GLB_TILE):
                if k0 + GLB_TILE < n + nc:
                    s_next = _dot_nt(q2, kd_scr[k0 + GLB_TILE:k0 + 2 * GLB_TILE, :])
                pv = _dot(jnp.exp(s - shift).astype(BF16), vd_scr[k0:k0 + GLB_TILE, :])
                acc = pv if acc is None else acc + pv
                s = s_next
            y_ref[0, rows, :] = _normalize_heads(acc).astype(y_ref.dtype)
            return carry

        lax.fori_loop(0, n_tiles, bounded_tile, 0, unroll=4)

    n_chunks = (n + nc) // GLB_KEYS

    def scores(q2, c):
        return _dot_nt(q2, kd_scr[c * GLB_KEYS:(c + 1) * GLB_KEYS, :])

    def stacked_q(t):
        rows = pl.ds(pl.multiple_of(t * tq, tq), tq)
        qn = normed_rope(q_ref[0, rows, :], qcos_ref[rows, :], qsin_ref[rows, :])
        return _stack_heads((qn * QK_SCALE).astype(BF16))

    def online_step(s, c, m, r):
        m_c = jnp.max(s, axis=-1, keepdims=True)
        m_new = m_c if m is None else jnp.maximum(m, m_c)
        pv = _dot(jnp.exp(s - m_new).astype(BF16), vd_scr[c * GLB_KEYS:(c + 1) * GLB_KEYS, :])
        return m_new, (pv if m is None else r * jnp.exp(m - m_new) + pv)

    def q_tile(t, carry):
        q2 = q2_scr[...]
        s = s_scr[...]
        m = r = None
        for c in range(n_chunks):
            if c + 1 < n_chunks:
                s_next = scores(q2, c + 1)
            else:
                q2_next = stacked_q(jnp.minimum(t + 1, n_tiles - 1))
                q2_scr[...] = q2_next
                s_next = scores(q2_next, 0)
            m, r = online_step(s, c, m, r)
            s = s_next
        s_scr[...] = s
        y_ref[0, pl.ds(pl.multiple_of(t * tq, tq), tq), :] = _normalize_heads(r).astype(y_ref.dtype)
        return carry

    if not bounded:
        q2_scr[...] = stacked_q(0)
        s_scr[...] = scores(q2_scr[...], 0)
        lax.fori_loop(0, n_tiles, q_tile, 0)

    if ctx_out:
        for t in range(nc // tq):
            rows = slice(t * tq, (t + 1) * tq)
            qn = _rms_heads(qc_ref[0, rows, :].astype(F32), bdm, qg)
            s = _dot_nt(_stack_heads((qn * QK_SCALE).astype(BF16)), kd_scr[0:nc, :])
            e = jnp.exp(s - jnp.max(s, axis=-1, keepdims=True))
            yc_ref[0, rows, :] = _normalize_heads(_dot(e.astype(BF16), vd_scr[0:nc, :])).astype(yc_ref.dtype)


def _global_gqa(p, pc, q_gain, k_gain, rope_tabs, ctx_out):
    b, n, _ = p.shape
    nc = pc.shape[1]
    tab = pl.BlockSpec((n, LANES), lambda i, j: (0, 0))
    gain = pl.BlockSpec((1, LANES), lambda i, j: (0, 0))
    kv = lambda rows, idx: pl.BlockSpec((1, rows, LANES), lambda i, j, idx=idx: (i, 0, idx))
    out_specs = [pl.BlockSpec((1, n, LANES), lambda i, j: (i, 0, j))]
    out_shape = [jax.ShapeDtypeStruct((b, n, GROUP_W), BF16)]
    if ctx_out:
        out_specs.append(pl.BlockSpec((1, nc, LANES), lambda i, j: (i, 0, j)))
        out_shape.append(jax.ShapeDtypeStruct((b, nc, GROUP_W), BF16))
    def call(bounded):
        return pl.pallas_call(
            functools.partial(_glb_kernel, ctx_out=ctx_out, bounded=bounded),
            grid=(b, 2),
            in_specs=[pl.BlockSpec(memory_space=pltpu.SMEM),
                      _lane_block(n, 0), kv(n, 2), kv(n, 3), _lane_block(nc, 0), kv(nc, 2), kv(nc, 3),
                      gain, gain, tab, tab, tab, tab],
            out_specs=out_specs,
            out_shape=out_shape,
            scratch_shapes=[pltpu.VMEM((n + nc, LANES), BF16)] * 2
            + ([pltpu.VMEM((n // CHUNK, 2 * CHUNK, LANES), BF16)] if bounded else
               [pltpu.VMEM((2 * CHUNK, LANES), BF16), pltpu.VMEM((2 * CHUNK, GLB_KEYS), F32)]),
            compiler_params=pltpu.CompilerParams(dimension_semantics=("arbitrary", "arbitrary"),
                                                 vmem_limit_bytes=VMEM_LIMIT),
            name="global_gqa" if bounded else "global_gqa_exact_max",
        )

    shift = (HEAD_DIM * QK_SCALE * GLB_BOUND_SLACK) * jnp.max(jnp.abs(q_gain)) * jnp.max(jnp.abs(k_gain))
    cos_tab, sin_tab = rope_tabs
    qg2, kg2 = jnp.tile(q_gain, 2)[None, :], jnp.tile(k_gain, 2)[None, :]
    partner = lambda g: g.reshape(1, -1, 2, ROPE_FREQS)[:, :, ::-1, :].reshape(1, LANES)
    args = (shift.reshape(1).astype(F32), p, p, p, pc, pc, pc, qg2, kg2,
            cos_tab * qg2, sin_tab * partner(qg2), cos_tab * kg2, sin_tab * partner(kg2))
    res = lax.cond(shift <= GLB_MAX_SHIFT, lambda a: call(True)(*a), lambda a: call(False)(*a), args)
    return res if ctx_out else (res[0], None)


def _win_kernel(sink_ref, q_ref, k_ref, v_ref, qc_ref, kc_ref, vc_ref,
                cos_ref, sin_ref, *rest, ctx_out):
    if ctx_out:
        y_ref, yc_ref, kd_scr, vd_scr, qa_scr = rest
    else:
        y_ref, kd_scr, vd_scr, qa_scr = rest
        yc_ref = None
    j = pl.program_id(1)
    n = q_ref.shape[1]
    nc = qc_ref.shape[1]
    tq = CHUNK
    nb = n // tq
    lat0 = nc + tq

    ones = jnp.ones((LANES, LANES), BF16)

    def sq_norms(xb, acc):
        xf = xb.astype(F32)
        return jnp.maximum(acc, _dot((xf * xf).astype(BF16), ones))

    zeros = jnp.zeros((tq, LANES), BF16)
    for r0 in (nc, lat0 + n):
        kd_scr[r0:r0 + tq, :] = zeros
        vd_scr[r0:r0 + tq, :] = zeros
    ksq_max = jnp.zeros((tq, LANES), F32)
    for t in range(nc // tq):
        rows = slice(t * tq, (t + 1) * tq)
        kd = _own_head_dup(kc_ref[0, rows, :].astype(F32), j).astype(BF16)
        kd_scr[rows, :] = kd
        ksq_max = sq_norms(kd, ksq_max)
        vd_scr[rows, :] = _values_with_ones(vc_ref[0, rows, :].astype(F32), j).astype(BF16)

    def prep(t, ksq_max):
        rows = pl.ds(pl.multiple_of(t * tq, tq), tq)
        dst = pl.ds(pl.multiple_of(lat0 + t * tq, tq), tq)
        kn = _rope_roll(k_ref[0, rows, :].astype(F32), cos_ref[rows, :], sin_ref[rows, :])
        kd = _own_head_dup(kn, j).astype(BF16)
        kd_scr[dst, :] = kd
        vd_scr[dst, :] = _values_with_ones(v_ref[0, rows, :].astype(F32), j).astype(BF16)
        return sq_norms(kd, ksq_max)

    ksq_max = lax.fori_loop(0, nb, prep, ksq_max, unroll=4)

    sk = jnp.where(_iota((2 * tq, 1), 0) < tq, sink_ref[2 * j], sink_ref[2 * j + 1])

    def stack_queries(t, qsq_max):
        rows = pl.ds(pl.multiple_of(t * tq, tq), tq)
        qn = _rope_roll(q_ref[0, rows, :].astype(F32), cos_ref[rows, :], sin_ref[rows, :])
        q2 = _stack_heads((qn * QK_SCALE).astype(BF16))
        qa_scr[t] = q2
        return sq_norms(q2, qsq_max)

    qsq_max = lax.fori_loop(0, nb, stack_queries, jnp.zeros((2 * tq, LANES), F32), unroll=4)
    bound_max = jnp.sqrt(jnp.max(qsq_max, keepdims=True) * (0.5 * jnp.max(ksq_max, keepdims=True))) * GLB_BOUND_SLACK
    shift_all = jnp.maximum(bound_max, sk)

    sum_lanes = _iota((2 * tq, LANES), 1) >= HEAD_DIM

    def softmax_pv(parts, sk, shift):
        if shift is None:
            shift = sk
            for s, _ in parts:
                shift = jnp.maximum(shift, jnp.max(s, axis=-1, keepdims=True))
        acc = jnp.where(sum_lanes, jnp.exp(sk - shift), 0.0)
        for s, vals in parts:
            acc = acc + _dot(jnp.exp(s - shift).astype(BF16), vals)
        return _normalize_heads(acc)

    band_c = _iota((2 * tq, 3 * tq), 1)
    off = band_c - _iota((2 * tq, 3 * tq), 0) % tq
    band_bias = jnp.where((off >= 0) & (off <= 2 * WINDOW), 0.0, NEG)
    col1 = _iota((1, 3 * tq), 1)
    first_bias = jnp.where(col1 < tq, NEG, 0.0)
    last_bias = jnp.where(col1 >= 2 * tq, NEG, 0.0)

    def q_tile(t, carry, bounded):
        rows = pl.ds(pl.multiple_of(t * tq, tq), tq)
        band = pl.ds(pl.multiple_of(nc + t * tq, tq), 3 * tq)
        q2 = qa_scr[t]
        s_ctx = _dot_nt(q2, kd_scr[0:nc, :])
        edge = jnp.where(t == 0, first_bias, 0.0) + jnp.where(t == nb - 1, last_bias, 0.0)
        s_band = _dot_nt(q2, kd_scr[band, :]) + (band_bias + edge)
        out = softmax_pv([(s_ctx, vd_scr[0:nc, :]), (s_band, vd_scr[band, :])], sk,
                         shift_all if bounded else None)
        y_ref[0, rows, :] = out.astype(y_ref.dtype)
        return carry

    small_logits = jnp.max(bound_max) <= GLB_MAX_SHIFT

    @pl.when(small_logits)
    def _():
        lax.fori_loop(0, nb, functools.partial(q_tile, bounded=True), 0, unroll=2)

    @pl.when(jnp.logical_not(small_logits))
    def _():
        lax.fori_loop(0, nb, functools.partial(q_tile, bounded=False), 0, unroll=2)

    if ctx_out:
        for t in range(nc // tq):
            rows = slice(t * tq, (t + 1) * tq)
            q2 = _stack_heads((qc_ref[0, rows, :].astype(F32) * QK_SCALE).astype(BF16))
            s_ctx = _dot_nt(q2, kd_scr[0:nc, :])
            yc_ref[0, rows, :] = softmax_pv([(s_ctx, vd_scr[0:nc, :])], sk, None).astype(yc_ref.dtype)


def _window_gqa(p, pc, sink, rope_tabs, ctx_out):
    b, n, _ = p.shape
    nc = pc.shape[1]
    tab = pl.BlockSpec((n, LANES), lambda i, j: (0, 0))
    kv = lambda rows, idx: pl.BlockSpec((1, rows, LANES), lambda i, j, idx=idx: (i, 0, idx))
    out_specs = [pl.BlockSpec((1, n, LANES), lambda i, j: (i, 0, j))]
    out_shape = [jax.ShapeDtypeStruct((b, n, GROUP_W), BF16)]
    if ctx_out:
        out_specs.append(pl.BlockSpec((1, nc, LANES), lambda i, j: (i, 0, j)))
        out_shape.append(jax.ShapeDtypeStruct((b, nc, GROUP_W), BF16))
    res = pl.pallas_call(
        functools.partial(_win_kernel, ctx_out=ctx_out),
        grid=(b, 2),
        in_specs=[pl.BlockSpec(memory_space=pltpu.SMEM),
                  _lane_block(n, 0), kv(n, 2), kv(n, 3), _lane_block(nc, 0), kv(nc, 2), kv(nc, 3),
                  tab, tab],
        out_specs=out_specs,
        out_shape=out_shape,
        scratch_shapes=[pltpu.VMEM((n + nc + 2 * CHUNK, LANES), BF16)] * 2
        + [pltpu.VMEM((n // CHUNK, 2 * CHUNK, LANES), BF16)],
        compiler_params=pltpu.CompilerParams(dimension_semantics=("arbitrary", "arbitrary"),
                                             vmem_limit_bytes=VMEM_LIMIT),
        name="window_gqa",
    )(sink, p, p, p, pc, pc, pc, *rope_tabs)
    return res if ctx_out else (res[0], None)


def _mid_rows(b, level, rev):
    c_len = b.shape[0]
    h = 1 << level
    mid_off = h if rev else h - 1
    if 2 * h >= 16:
        pieces = [jnp.broadcast_to(b[m * 2 * h + mid_off:m * 2 * h + mid_off + 1, :], (2 * h, b.shape[1]))
                  for m in range(c_len // (2 * h))]
        return pieces[0] if len(pieces) == 1 else jnp.concatenate(pieces, axis=0)
    r = _iota(b.shape, 0) % (2 * h)
    out = b
    for rr in range(2 * h):
        if rr != mid_off:
            out = jnp.where(r == rr, pltpu.roll(b, (rr - mid_off) % c_len, 0), out)
    return out


def _level_masks():
    t = np.arange(2 * CHUNK)[:, None] % CHUNK
    s = np.arange(CHUNK)[None, :]
    out = np.zeros((2, N_LEVELS, 2 * CHUNK, CHUNK), np.float32)
    for level in range(N_LEVELS):
        same = (t >> (level + 1)) == (s >> (level + 1))
        t_hi, s_hi = (t >> level) & 1, (s >> level) & 1
        out[0, level] = same & (t_hi == 1) & (s_hi == 0)
        out[1, level] = same & (t_hi == 0) & (s_hi == 1)
    return out


def _hg_kernel(lvl_ref, lb_ref, q_ref, zf_ref, zb_ref, i_ref, g_ref, qc_ref, zfc_ref, zbc_ref, ic_ref, gc_ref,
               *rest, ctx_out):
    if ctx_out:
        y_ref, yc_ref, o_scr = rest
    else:
        y_ref, o_scr = rest
        yc_ref = None
    c_len = CHUNK
    n_lat = q_ref.shape[1] // c_len
    n_ctx = qc_ref.shape[1] // c_len
    lb = lb_ref[0]
    bd = _head_block_mask()
    bd_ones = jnp.where(bd, 1.0, 0.0).astype(BF16)
    bdm = jnp.where(bd, 1.0 / HEAD_DIM, 0.0).astype(BF16)
    rowi = _iota((c_len, c_len), 0)
    coli = _iota((c_len, c_len), 1)
    tri = (jnp.where(coli <= rowi, 1.0, 0.0).astype(BF16), jnp.where(coli >= rowi, 1.0, 0.0).astype(BF16))

    def step(qraw, z, v, st, rev, need_out):
        sg = jax.nn.sigmoid(z.astype(F32))
        f = lb + (1.0 - lb) * sg
        x = jnp.log(jnp.maximum(f, TINY))
        kk = (1.0 - lb) * (1.0 - sg)
        b = _split_dot_lhs(tri[1 if rev else 0], x)
        edge = b[0:1, :] if rev else b[c_len - 1:c_len, :]
        o = None
        if need_out:
            qs = _silu(qraw.astype(F32))
            o = _dot_nt((qs * jnp.exp(b)).astype(BF16), st.astype(BF16))
            o = o + _split_dot(qs * kk, bd_ones) * v.astype(F32)
            a = None
            for level in range(N_LEVELS):
                e = jnp.exp(-jnp.abs(b - _mid_rows(b, level, rev)))
                al = _dot_nt(_stack_heads((qs * e).astype(BF16)), (kk * e).astype(BF16))
                al = al * lvl_ref[1 if rev else 0, level]
                a = al if a is None else a + al
            o = o + _unstack_heads(_dot(a.astype(BF16), v))
        u = _dot_tn(v, (kk * jnp.exp(edge - b)).astype(BF16))
        return o, st * jnp.exp(edge) + jnp.where(bd, u, 0.0)

    def finish(o, g):
        return o * lax.rsqrt(_dot((o * o).astype(BF16), bdm) + EPS) * _silu(g.astype(F32))

    for rev in (False, True):
        z_ref, zc_ref = (zb_ref, zbc_ref) if rev else (zf_ref, zfc_ref)
        st = jnp.zeros((LANES, LANES), F32)
        for c in (reversed(range(n_ctx)) if rev else range(n_ctx)):
            rows = slice(c * c_len, (c + 1) * c_len)
            o, st = step(qc_ref[0, rows, :], zc_ref[0, rows, :], ic_ref[0, rows, :], st, rev, ctx_out)
            if ctx_out:
                if rev:
                    yc_ref[0, rows, :] = finish(o_scr[rows, :] + o, gc_ref[0, rows, :]).astype(yc_ref.dtype)
                else:
                    o_scr[rows, :] = o

        def lat(i, st, rev=rev, z_ref=z_ref):
            ci = (n_lat - 1 - i) if rev else i
            rows = pl.ds(pl.multiple_of(ci * c_len, c_len), c_len)
            srows = pl.ds(pl.multiple_of((ci + n_ctx) * c_len, c_len), c_len)
            o, st = step(q_ref[0, rows, :], z_ref[0, rows, :], i_ref[0, rows, :], st, rev, True)
            if rev:
                y_ref[0, rows, :] = finish(o_scr[srows, :] + o, g_ref[0, rows, :]).astype(y_ref.dtype)
            else:
                o_scr[srows, :] = o
            return st

        lax.fori_loop(0, n_lat, lat, st, unroll=2)


def _split_dot_lhs(w, x):
    hi = x.astype(BF16)
    lo = (x - hi.astype(F32)).astype(BF16)
    return _dot(w, hi) + _dot(w, lo)


def _hgrn2(p, pc, lower_bound, ctx_out):
    b, n, _ = p.shape
    nc = pc.shape[1]
    out_specs = [pl.BlockSpec((1, n, LANES), lambda i, j: (i, 0, j))]
    out_shape = [jax.ShapeDtypeStruct((b, n, GROUP_W), BF16)]
    if ctx_out:
        out_specs.append(pl.BlockSpec((1, nc, LANES), lambda i, j: (i, 0, j)))
        out_shape.append(jax.ShapeDtypeStruct((b, nc, GROUP_W), BF16))
    res = pl.pallas_call(
        functools.partial(_hg_kernel, ctx_out=ctx_out),
        grid=(b, 2),
        in_specs=[pl.BlockSpec((2, N_LEVELS, 2 * CHUNK, CHUNK), lambda i, j: (0, 0, 0, 0)),
                  pl.BlockSpec((1, 1, LANES), lambda i, j: (j, 0, 0))]
        + [_lane_block(n, 2 * s) for s in range(5)] + [_lane_block(nc, 2 * s) for s in range(5)],
        out_specs=out_specs,
        out_shape=out_shape,
        scratch_shapes=[pltpu.VMEM((n + nc, LANES), F32)],
        compiler_params=pltpu.CompilerParams(dimension_semantics=("arbitrary", "arbitrary"),
                                             vmem_limit_bytes=VMEM_LIMIT),
        name="hgrn2",
    )(jnp.asarray(_level_masks()), lower_bound.reshape(2, 1, LANES), p, p, p, p, p, pc, pc, pc, pc, pc)
    return res if ctx_out else (res[0], None)


FF_COLS = 1024


def _out_mlp_kernel(x_ref, yr_ref, yg_ref, yw_ref, yh_ref, mod_ref, wo_ref, wu_ref, wd_ref,
                    g1_ref, b1_ref, g2_ref, b2_ref, o_ref, h_scr):
    half = x_ref.shape[1] // 2
    n_ff = wu_ref.shape[1] // FF_COLS
    halves = (slice(0, half), slice(half, 2 * half))

    def mixer_proj(r):
        y = None
        for s, y_ref in enumerate((yr_ref, yg_ref, yw_ref, yh_ref)):
            part = _dot(y_ref[0, r, :], wo_ref[s * GROUP_W:(s + 1) * GROUP_W, :])
            y = part if y is None else y + part
        return y

    def post_mixer(r, y):
        x1 = _ln_rows(ALPHA * x_ref[0, r, :] + mod_ref[0, 2:3, :] * y) * g1_ref[...] + b1_ref[...]
        o_ref[0, r, :] = x1
        h_scr[r, :] = (_ln_rows(x1) * (1.0 + mod_ref[0, 4:5, :]) + mod_ref[0, 3:4, :]).astype(BF16)

    def up(r, f):
        return _dot(h_scr[r, :], wu_ref[:, f * FF_COLS:(f + 1) * FF_COLS])

    def down(u, f):
        u = jnp.maximum(u, 0.0)
        return _dot((u * u).astype(BF16), wd_ref[f * FF_COLS:(f + 1) * FF_COLS, :])

    def post_mlp(r, acc):
        o_ref[0, r, :] = _ln_rows(ALPHA * o_ref[0, r, :] + mod_ref[0, 5:6, :] * acc) * g2_ref[...] + b2_ref[...]

    ys = [mixer_proj(r) for r in halves]
    post_mixer(halves[0], ys[0])
    units = [(r, f) for r in halves for f in range(n_ff)]
    u_next = up(*units[0])
    post_mixer(halves[1], ys[1])
    acc = None
    for k, (r, f) in enumerate(units):
        u = u_next
        if k + 1 < len(units):
            u_next = up(*units[k + 1])
        part = down(u, f)
        acc = part if f == 0 else acc + part
        if f == n_ff - 1:
            post_mlp(r, acc)


def _out_mlp(x, ys, mod, w_out, w_up, w_down, g1, b1, g2, b2, tm, shared_mod):
    b, n, d = x.shape
    mod_map = (lambda i, t: (0, 0, 0)) if shared_mod else (lambda i, t: (i, 0, 0))
    full = lambda a: pl.BlockSpec(a.shape, lambda i, t: (0,) * a.ndim, pipeline_mode=pl.Buffered(1))
    vecs = [v.reshape(1, d) for v in (g1, b1, g2, b2)]
    return pl.pallas_call(
        _out_mlp_kernel,
        grid=(b, n // tm),
        in_specs=[pl.BlockSpec((1, tm, d), lambda i, t: (i, t, 0))]
        + [pl.BlockSpec((1, tm, GROUP_W), lambda i, t: (i, t, 0))] * 4
        + [pl.BlockSpec((1, 6, d), mod_map), full(w_out), full(w_up), full(w_down)]
        + [full(v) for v in vecs],
        out_specs=pl.BlockSpec((1, tm, d), lambda i, t: (i, t, 0)),
        out_shape=jax.ShapeDtypeStruct((b, n, d), F32),
        scratch_shapes=[pltpu.VMEM((tm, d), BF16)],
        compiler_params=pltpu.CompilerParams(dimension_semantics=("arbitrary", "arbitrary"),
                                             vmem_limit_bytes=VMEM_LIMIT),
        name="out_proj_mlp",
    )(x, *ys, mod, w_out, w_up, w_down, *vecs)


def _rope_tables(n):
    rows = n // GRID_W
    row = jnp.repeat(jnp.arange(rows), GRID_W).astype(F32)
    col = (jnp.arange(rows * GRID_W) % GRID_W).astype(F32)
    inv = ROPE_THETA ** (-jnp.arange(ROPE_FREQS, dtype=F32) / ROPE_FREQS)
    cos, sin = [], []
    for pos in (row, col):
        ang = pos[:, None] * inv
        cos += [jnp.cos(ang), jnp.cos(ang)]
        sin += [-jnp.sin(ang), jnp.sin(ang)]
    return tuple(jnp.tile(jnp.concatenate(t, axis=1), (1, LANES // HEAD_DIM)) for t in (cos, sin))


def kernel(x, c, ctx, c_ctx, w_ada, b_ada, w_in, ret_decay_logit, gqa_q_gain, gqa_k_gain,
           swa_sink, hgrn_lb, w_out, ln1_g, ln1_b, w_up, w_down, ln2_g, ln2_b):
    batch, n, d = x.shape
    depth = w_ada.shape[0]
    rope_tabs = _rope_tables(n)
    p_lb = jax.nn.softmax(hgrn_lb.astype(F32), axis=0)
    lower_bounds = jnp.cumsum(p_lb, axis=0) - p_lb[0]
    log_decay = jax.nn.log_sigmoid(ret_decay_logit.astype(F32))

    pad = (-(batch + 1)) % 8
    cvec = jnp.concatenate([c, c_ctx[None, :], jnp.zeros((pad, d), F32)], axis=0)
    mods = _modulation(cvec, w_ada, b_ada)
    w_in_bf, w_out_bf, w_up_bf, w_down_bf = (w.astype(BF16) for w in (w_in, w_out, w_up, w_down))

    xc = ctx
    for l in range(depth):
        ctx_out = l < depth - 1
        m_lat = mods[l, :batch].reshape(batch, 6, d)
        m_ctx = mods[l, batch:batch + 1].reshape(1, 6, d)
        p = _project(x, m_lat, w_in_bf[l], 1024, False)
        pc = _project(xc, m_ctx, w_in_bf[l], xc.shape[1], True)
        y_ret, yc_ret = _retention(p[0], pc[0], log_decay[l], rope_tabs, ctx_out)
        y_glb, yc_glb = _global_gqa(p[1], pc[1], gqa_q_gain[l], gqa_k_gain[l], rope_tabs, ctx_out)
        y_win, yc_win = _window_gqa(p[2], pc[2], swa_sink[l], rope_tabs, ctx_out)
        y_hg, yc_hg = _hgrn2(p[3], pc[3], lower_bounds[l], ctx_out)
        post = (w_out_bf[l], w_up_bf[l], w_down_bf[l], ln1_g[l], ln1_b[l], ln2_g[l], ln2_b[l])
        x = _out_mlp(x, (y_ret, y_glb, y_win, y_hg), m_lat, *post, 1024, False)
        if ctx_out:
            xc = _out_mlp(xc, (yc_ret, yc_glb, yc_win, yc_hg), m_ctx, *post, 256, True)
    return x
```

```python
import functools

import numpy as np
import jax
import jax.numpy as jnp
from jax import lax
from jax.experimental import pallas as pl
from jax.experimental.pallas import tpu as pltpu

F32 = jnp.float32
BF16 = jnp.bfloat16

D_MODEL = 1024
CTX_LEN = 256
GRID_W = 64
HEAD_DIM = 64
GROUP_W = 256
D_FF = 4 * D_MODEL
DEPTH = 2
ROPE_THETA = 10000.0
ROPE_FREQS = HEAD_DIM // 4
WINDOW = 128
ALPHA = (2 * DEPTH) ** 0.25
EPS = 1e-6
NEG = -1e30
TINY = 1e-30
QK_SCALE = HEAD_DIM ** -0.5

LANES = 128
CHUNK = 128
N_LEVELS = 7
VMEM_LIMIT = 56 * 1024 * 1024


def _dot(a, b):
    return jnp.dot(a, b, preferred_element_type=F32)


def _dot_nt(a, b):
    return lax.dot_general(a, b, (((1,), (1,)), ((), ())), preferred_element_type=F32)


def _dot_tn(a, b):
    return lax.dot_general(a, b, (((0,), (0,)), ((), ())), preferred_element_type=F32)


def _split_dot(x, w):
    hi = x.astype(BF16)
    lo = (x - hi.astype(F32)).astype(BF16)
    return _dot(hi, w) + _dot(lo, w)


def _silu(x):
    return x * jax.nn.sigmoid(x)


def _ln_rows(x):
    mu = jnp.mean(x, axis=-1, keepdims=True)
    xc = x - mu
    var = jnp.mean(xc * xc, axis=-1, keepdims=True)
    return xc * lax.rsqrt(var + EPS)


def _rope_roll(x, c, sn):
    first = (_iota(x.shape, 1) & ROPE_FREQS) == 0
    partner = jnp.where(first, pltpu.roll(x, LANES - ROPE_FREQS, 1), pltpu.roll(x, ROPE_FREQS, 1))
    return x * c + partner * sn


def _iota(shape, dim):
    return lax.broadcasted_iota(jnp.int32, shape, dim)


def _head_block_mask():
    return (_iota((LANES, LANES), 0) // HEAD_DIM) == (_iota((LANES, LANES), 1) // HEAD_DIM)


def _stack_heads(qb):
    lo = _iota(qb.shape, 1) < HEAD_DIM
    zero = jnp.zeros_like(qb)
    return jnp.concatenate([jnp.where(lo, qb, zero), jnp.where(lo, zero, qb)], axis=0)


def _unstack_heads(r):
    t = r.shape[0] // 2
    lo = _iota((t, LANES), 1) < HEAD_DIM
    return jnp.where(lo, r[:t], r[t:])


def _mod_kernel(c_ref, w_ref, b_ref, o_ref):
    s = _silu(c_ref[...])
    o_ref[0] = _dot(s.astype(BF16), w_ref[0].astype(BF16)) + b_ref[0]


def _modulation(cvec, w_ada, b_ada):
    rows = cvec.shape[0]
    depth, d, n = w_ada.shape
    nt = n // d
    return pl.pallas_call(
        _mod_kernel,
        grid=(depth, nt),
        in_specs=[pl.BlockSpec((rows, d), lambda l, j: (0, 0)),
                  pl.BlockSpec((1, d, d), lambda l, j: (l, 0, j)),
                  pl.BlockSpec((1, 1, d), lambda l, j: (l, 0, j))],
        out_specs=pl.BlockSpec((1, rows, d), lambda l, j: (l, 0, j)),
        out_shape=jax.ShapeDtypeStruct((depth, rows, n), F32),
        compiler_params=pltpu.CompilerParams(dimension_semantics=("arbitrary", "arbitrary"),
                                             vmem_limit_bytes=VMEM_LIMIT),
        name="modulation",
    )(cvec, w_ada, b_ada.reshape(depth, 1, n))


PROJ_GROUPS = (4 * GROUP_W, 2 * GROUP_W, 2 * GROUP_W, 5 * GROUP_W)
PROJ_COLS = 256


def _proj_kernel(x_ref, mod_ref, w_ref, o_ret, o_glb, o_win, o_hg, h_scr):
    half = x_ref.shape[1] // 2
    halves = (slice(0, half), slice(half, 2 * half))

    def modulate(r):
        h_scr[r, :] = (_ln_rows(x_ref[0, r, :]) * (1.0 + mod_ref[0, 1:2, :]) + mod_ref[0, 0:1, :]).astype(BF16)

    def project(r, first_chunk_only):
        col = 0
        for o_ref, width in zip((o_ret, o_glb, o_win, o_hg), PROJ_GROUPS):
            for j in range(width // PROJ_COLS):
                if first_chunk_only == (col == 0):
                    o_ref[0, r, j * PROJ_COLS:(j + 1) * PROJ_COLS] = _dot(
                        h_scr[r, :], w_ref[:, col:col + PROJ_COLS]).astype(BF16)
                col += PROJ_COLS

    modulate(halves[0])
    project(halves[0], True)
    modulate(halves[1])
    project(halves[0], False)
    project(halves[1], True)
    project(halves[1], False)


def _project(x, mod, w_in_bf, tm, shared_mod):
    b, n, d = x.shape
    n_in = w_in_bf.shape[1]
    mod_map = (lambda i, t: (0, 0, 0)) if shared_mod else (lambda i, t: (i, 0, 0))
    return pl.pallas_call(
        _proj_kernel,
        grid=(b, n // tm),
        in_specs=[pl.BlockSpec((1, tm, d), lambda i, t: (i, t, 0)),
                  pl.BlockSpec((1, 6, d), mod_map),
                  pl.BlockSpec((d, n_in), lambda i, t: (0, 0), pipeline_mode=pl.Buffered(1))],
        out_specs=[pl.BlockSpec((1, tm, w), lambda i, t: (i, t, 0)) for w in PROJ_GROUPS],
        out_shape=[jax.ShapeDtypeStruct((b, n, w), BF16) for w in PROJ_GROUPS],
        scratch_shapes=[pltpu.VMEM((tm, d), BF16)],
        compiler_params=pltpu.CompilerParams(dimension_semantics=("arbitrary", "arbitrary"),
                                             vmem_limit_bytes=VMEM_LIMIT),
        name="adaln_in_proj",
    )(x, mod, w_in_bf)


RET_GROUP = 8


def _ret_kernel(lg_ref, q_ref, k_ref, v_ref, g_ref, qc_ref, kc_ref, vc_ref, gc_ref,
                cos_ref, sin_ref, *rest, ctx_out):
    if ctx_out:
        y_ref, yc_ref, o_scr, qs_scr, ks_scr = rest
    else:
        y_ref, o_scr, qs_scr, ks_scr = rest
        yc_ref = None
    j = pl.program_id(1)
    c_len = CHUNK
    n_lat = q_ref.shape[1] // c_len
    n_ctx = qc_ref.shape[1] // c_len

    hi1 = _iota((1, LANES), 1) >= HEAD_DIM
    rowi = _iota((c_len, LANES), 0)
    coli = _iota((c_len, LANES), 1)
    pos = rowi.astype(F32)
    lgf = jnp.where(hi1, lg_ref[0, 2 * j + 1], lg_ref[0, 2 * j])
    lgb = jnp.where(hi1, lg_ref[1, 2 * j + 1], lg_ref[1, 2 * j])
    wqf = jnp.exp(lgf * (pos + 1.0))
    wqb = jnp.exp(lgb * (c_len - pos))
    wkf = jnp.exp(lgf * (c_len - 1.0 - pos))
    wkb = jnp.exp(lgb * pos)
    dcf = jnp.exp(lgf * c_len)
    dcb = jnp.exp(lgb * c_len)
    dlt = (rowi - coli).astype(F32)

    def head_mask(e):
        lf = lg_ref[0, 2 * j + e]
        lb = lg_ref[1, 2 * j + e]
        return (jnp.where(dlt >= 0, jnp.exp(lf * jnp.maximum(dlt, 0.0)), 0.0)
                + jnp.where(dlt <= 0, jnp.exp(lb * jnp.maximum(-dlt, 0.0)), 0.0))

    m2 = jnp.concatenate([head_mask(0), head_mask(1)], axis=0)
    bd = _head_block_mask()
    bdm = jnp.where(bd, 1.0 / HEAD_DIM, 0.0).astype(BF16)


    def state_chain(st, us, decay):
        seen = []
        for u in us:
            seen.append(st)
            st = st * decay + jnp.where(bd, u, 0.0)
        return seen, st

    def fwd_group(cs, qkv, st, need_out):
        k8s = [k * QK_SCALE for _, k, _ in qkv]
        us = [_dot_tn(v, (k8 * wkf).astype(BF16)) for (_, _, v), k8 in zip(qkv, k8s)]
        seen, st = state_chain(st, us, dcf)
        if need_out:
            a2s = [_dot_nt(_stack_heads(q.astype(BF16)), k8.astype(BF16)) for (q, _, _), k8 in zip(qkv, k8s)]
            rs = [_dot((a2 * m2).astype(BF16), v) for a2, (_, _, v) in zip(a2s, qkv)]
            inter = [_dot_nt((q * wqf).astype(BF16), s.astype(BF16)) for (q, _, _), s in zip(qkv, seen)]
            for c, r, x in zip(cs, rs, inter):
                o_scr[pl.ds(c * c_len, c_len), :] = _unstack_heads(r) + x
        for c, (q, _, _), k8 in zip(cs, qkv, k8s):
            qs_scr[pl.ds(c * c_len, c_len), :] = q
            ks_scr[pl.ds(c * c_len, c_len), :] = k8
        return st

    def bwd_group(cs, vgs, st, out_ref, row0s, need_out):
        qs = [qs_scr[pl.ds(c * c_len, c_len), :] for c in cs]
        k8s = [ks_scr[pl.ds(c * c_len, c_len), :] for c in cs]
        us = [_dot_tn(v, (k8 * wkb).astype(BF16)) for (v, _), k8 in zip(vgs, k8s)]
        seen, st = state_chain(st, us, dcb)
        if need_out:
            os_ = [o_scr[pl.ds(c * c_len, c_len), :] + _dot_nt((q * wqb).astype(BF16), s.astype(BF16))
                   for c, q, s in zip(cs, qs, seen)]
            xcs = [o - _split_dot(o, bdm) for o in os_]
            vars_ = [_dot((xc * xc).astype(BF16), bdm) for xc in xcs]
            for row0, xc, var, (_, g) in zip(row0s, xcs, vars_, vgs):
                out_ref[0, pl.ds(row0, c_len), :] = (xc * lax.rsqrt(var + EPS) * _silu(g)).astype(out_ref.dtype)
        return st

    ctx_rows = [slice(c * c_len, (c + 1) * c_len) for c in range(n_ctx)]
    st = fwd_group(list(range(n_ctx)),
                   [(qc_ref[0, r, :].astype(F32), kc_ref[0, r, :].astype(F32), vc_ref[0, r, :]) for r in ctx_rows],
                   jnp.zeros((LANES, LANES), F32), ctx_out)

    def lat_fwd(i, st):
        idx = [i * RET_GROUP + k for k in range(RET_GROUP)]
        qkv = []
        for ci in idx:
            rows = pl.ds(pl.multiple_of(ci * c_len, c_len), c_len)
            cs, sn = cos_ref[rows, :], sin_ref[rows, :]
            qkv.append((_rope_roll(q_ref[0, rows, :].astype(F32), cs, sn),
                        _rope_roll(k_ref[0, rows, :].astype(F32), cs, sn), v_ref[0, rows, :]))
        return fwd_group([ci + n_ctx for ci in idx], qkv, st, True)

    lax.fori_loop(0, n_lat // RET_GROUP, lat_fwd, st)

    st = bwd_group(list(reversed(range(n_ctx))),
                   [(vc_ref[0, r, :], gc_ref[0, r, :].astype(F32) if ctx_out else None) for r in reversed(ctx_rows)],
                   jnp.zeros((LANES, LANES), F32), yc_ref, [c * c_len for c in reversed(range(n_ctx))], ctx_out)

    def lat_bwd(i, st):
        idx = [n_lat - 1 - i * RET_GROUP - k for k in range(RET_GROUP)]
        row0s = [pl.multiple_of(ci * c_len, c_len) for ci in idx]
        vgs = [(v_ref[0, pl.ds(r0, c_len), :], g_ref[0, pl.ds(r0, c_len), :].astype(F32)) for r0 in row0s]
        return bwd_group([ci + n_ctx for ci in idx], vgs, st, y_ref, row0s, True)

    lax.fori_loop(0, n_lat // RET_GROUP, lat_bwd, st)


def _lane_block(n, idx):
    return pl.BlockSpec((1, n, LANES), lambda i, j, idx=idx: (i, 0, idx + j))


def _retention(p, pc, log_decay, rope_tabs, ctx_out):
    b, n, _ = p.shape
    nc = pc.shape[1]
    tab = pl.BlockSpec((n, LANES), lambda i, j: (0, 0))
    out_specs = [pl.BlockSpec((1, n, LANES), lambda i, j: (i, 0, j))]
    out_shape = [jax.ShapeDtypeStruct((b, n, GROUP_W), BF16)]
    if ctx_out:
        out_specs.append(pl.BlockSpec((1, nc, LANES), lambda i, j: (i, 0, j)))
        out_shape.append(jax.ShapeDtypeStruct((b, nc, GROUP_W), BF16))
    res = pl.pallas_call(
        functools.partial(_ret_kernel, ctx_out=ctx_out),
        grid=(b, 2),
        in_specs=[pl.BlockSpec(memory_space=pltpu.SMEM)]
        + [_lane_block(n, 2 * s) for s in range(4)] + [_lane_block(nc, 2 * s) for s in range(4)]
        + [tab, tab],
        out_specs=out_specs,
        out_shape=out_shape,
        scratch_shapes=[pltpu.VMEM((n + nc, LANES), F32)] * 3,
        compiler_params=pltpu.CompilerParams(dimension_semantics=("arbitrary", "arbitrary"),
                                             vmem_limit_bytes=VMEM_LIMIT),
        name="retention",
    )(log_decay, p, p, p, p, pc, pc, pc, pc, *rope_tabs)
    return res if ctx_out else (res[0], None)


def _own_head_dup(x, j):
    own = (_iota(x.shape, 1) // HEAD_DIM) == j
    return jnp.where(own, x, pltpu.roll(x, HEAD_DIM, 1))


def _values_with_ones(v, j):
    lo = _iota(v.shape, 1) < HEAD_DIM
    return jnp.where(lo, _own_head_dup(v, j), 1.0)


def _normalize_heads(r):
    t = r.shape[0] // 2
    top, bot = r[:t], r[t:]
    lo = _iota((t, LANES), 1) < HEAD_DIM
    return jnp.where(lo, top / pltpu.roll(top, HEAD_DIM, 1), pltpu.roll(bot, HEAD_DIM, 1) / bot)


GLB_KEYS = 768
GLB_TILE = 256
GLB_BOUND_SLACK = 1.01
GLB_MAX_SHIFT = 40.0


def _rms_heads(x, bdm, gain):
    return x * lax.rsqrt(_split_dot(x * x, bdm) + EPS) * gain


def _glb_kernel(shift_ref, q_ref, k_ref, v_ref, qc_ref, kc_ref, vc_ref, qg_ref, kg_ref,
                qcos_ref, qsin_ref, kcos_ref, ksin_ref, *rest, ctx_out, bounded):
    n = q_ref.shape[1]
    nc = qc_ref.shape[1]
    if ctx_out:
        y_ref, yc_ref, kd_scr, vd_scr = rest[:4]
    else:
        y_ref, kd_scr, vd_scr = rest[:3]
        yc_ref = None
    if bounded:
        qa_scr, = rest[-1:]
    else:
        q2_scr, s_scr = rest[-2:]
    j = pl.program_id(1)
    tq = CHUNK
    bdm = jnp.where(_head_block_mask(), 1.0 / HEAD_DIM, 0.0).astype(BF16)
    qg = qg_ref[...]
    kg = kg_ref[...]
    def normed_rope(xb, cos_tab, sin_tab):
        x = xb.astype(F32)
        return lax.rsqrt(_split_dot(x * x, bdm) + EPS) * _rope_roll(x, cos_tab, sin_tab)

    for t in range(nc // tq):
        rows = slice(t * tq, (t + 1) * tq)
        kn = _rms_heads(kc_ref[0, rows, :].astype(F32), bdm, kg)
        kd_scr[rows, :] = _own_head_dup(kn, j).astype(BF16)
        vd_scr[rows, :] = _values_with_ones(vc_ref[0, rows, :].astype(F32), j).astype(BF16)

    def prep(t, carry):
        rows = pl.ds(pl.multiple_of(t * tq, tq), tq)
        dst = pl.ds(pl.multiple_of(nc + t * tq, tq), tq)
        x = k_ref[0, rows, :].astype(F32)
        kn = lax.rsqrt(_split_dot(x * x, bdm) + EPS) * _rope_roll(x, kcos_ref[rows, :], ksin_ref[rows, :])
        kd_scr[dst, :] = _own_head_dup(kn, j).astype(BF16)
        vd_scr[dst, :] = _values_with_ones(v_ref[0, rows, :].astype(F32), j).astype(BF16)
        return carry

    lax.fori_loop(0, n // tq, prep, 0, unroll=4)

    n_tiles = n // tq

    if bounded:
        shift = shift_ref[0]

        def stack_queries(t, carry):
            rows = pl.ds(pl.multiple_of(t * tq, tq), tq)
            qn = normed_rope(q_ref[0, rows, :], qcos_ref[rows, :], qsin_ref[rows, :])
            qa_scr[t] = _stack_heads((qn * QK_SCALE).astype(BF16))
            return carry

        lax.fori_loop(0, n_tiles, stack_queries, 0, unroll=4)

        def bounded_tile(t, carry):
            rows = pl.ds(pl.multiple_of(t * tq, tq), tq)
            q2 = qa_scr[t]
            acc = None
            s = _dot_nt(q2, kd_scr[0:GLB_TILE, :])
            for k0 in range(0, n + nc, GLB_TILE):
                if k0 + GLB_TILE < n + nc:
                    s_next = _dot_nt(q2, kd_scr[k0 + GLB_TILE:k0 + 2 * GLB_TILE, :])
                pv = _dot(jnp.exp(s - shift).astype(BF16), vd_scr[k0:k0 + GLB_TILE, :])
                acc = pv if acc is None else acc + pv
                s = s_next
            y_ref[0, rows, :] = _normalize_heads(acc).astype(y_ref.dtype)
            return carry

        lax.fori_loop(0, n_tiles, bounded_tile, 0, unroll=4)

    n_chunks = (n + nc) // GLB_KEYS

    def scores(q2, c):
        return _dot_nt(q2, kd_scr[c * GLB_KEYS:(c + 1) * GLB_KEYS, :])

    def stacked_q(t):
        rows = pl.ds(pl.multiple_of(t * tq, tq), tq)
        qn = normed_rope(q_ref[0, rows, :], qcos_ref[rows, :], qsin_ref[rows, :])
        return _stack_heads((qn * QK_SCALE).astype(BF16))

    def online_step(s, c, m, r):
        m_c = jnp.max(s, axis=-1, keepdims=True)
        m_new = m_c if m is None else jnp.maximum(m, m_c)
        pv = _dot(jnp.exp(s - m_new).astype(BF16), vd_scr[c * GLB_KEYS:(c + 1) * GLB_KEYS, :])
        return m_new, (pv if m is None else r * jnp.exp(m - m_new) + pv)

    def q_tile(t, carry):
        q2 = q2_scr[...]
        s = s_scr[...]
        m = r = None
        for c in range(n_chunks):
            if c + 1 < n_chunks:
                s_next = scores(q2, c + 1)
            else:
                q2_next = stacked_q(jnp.minimum(t + 1, n_tiles - 1))
                q2_scr[...] = q2_next
                s_next = scores(q2_next, 0)
            m, r = online_step(s, c, m, r)
            s = s_next
        s_scr[...] = s
        y_ref[0, pl.ds(pl.multiple_of(t * tq, tq), tq), :] = _normalize_heads(r).astype(y_ref.dtype)
        return carry

    if not bounded:
        q2_scr[...] = stacked_q(0)
        s_scr[...] = scores(q2_scr[...], 0)
        lax.fori_loop(0, n_tiles, q_tile, 0)

    if ctx_out:
        for t in range(nc // tq):
            rows = slice(t * tq, (t + 1) * tq)
            qn = _rms_heads(qc_ref[0, rows, :].astype(F32), bdm, qg)
            s = _dot_nt(_stack_heads((qn * QK_SCALE).astype(BF16)), kd_scr[0:nc, :])
            e = jnp.exp(s - jnp.max(s, axis=-1, keepdims=True))
            yc_ref[0, rows, :] = _normalize_heads(_dot(e.astype(BF16), vd_scr[0:nc, :])).astype(yc_ref.dtype)


def _global_gqa(p, pc, q_gain, k_gain, rope_tabs, ctx_out):
    b, n, _ = p.shape
    nc = pc.shape[1]
    tab = pl.BlockSpec((n, LANES), lambda i, j: (0, 0))
    gain = pl.BlockSpec((1, LANES), lambda i, j: (0, 0))
    kv = lambda rows, idx: pl.BlockSpec((1, rows, LANES), lambda i, j, idx=idx: (i, 0, idx))
    out_specs = [pl.BlockSpec((1, n, LANES), lambda i, j: (i, 0, j))]
    out_shape = [jax.ShapeDtypeStruct((b, n, GROUP_W), BF16)]
    if ctx_out:
        out_specs.append(pl.BlockSpec((1, nc, LANES), lambda i, j: (i, 0, j)))
        out_shape.append(jax.ShapeDtypeStruct((b, nc, GROUP_W), BF16))
    def call(bounded):
        return pl.pallas_call(
            functools.partial(_glb_kernel, ctx_out=ctx_out, bounded=bounded),
            grid=(b, 2),
            in_specs=[pl.BlockSpec(memory_space=pltpu.SMEM),
                      _lane_block(n, 0), kv(n, 2), kv(n, 3), _lane_block(nc, 0), kv(nc, 2), kv(nc, 3),
                      gain, gain, tab, tab, tab, tab],
            out_specs=out_specs,
            out_shape=out_shape,
            scratch_shapes=[pltpu.VMEM((n + nc, LANES), BF16)] * 2
            + ([pltpu.VMEM((n // CHUNK, 2 * CHUNK, LANES), BF16)] if bounded else
               [pltpu.VMEM((2 * CHUNK, LANES), BF16), pltpu.VMEM((2 * CHUNK, GLB_KEYS), F32)]),
            compiler_params=pltpu.CompilerParams(dimension_semantics=("arbitrary", "arbitrary"),
                                                 vmem_limit_bytes=VMEM_LIMIT),
            name="global_gqa" if bounded else "global_gqa_exact_max",
        )

    shift = (HEAD_DIM * QK_SCALE * GLB_BOUND_SLACK) * jnp.max(jnp.abs(q_gain)) * jnp.max(jnp.abs(k_gain))
    cos_tab, sin_tab = rope_tabs
    qg2, kg2 = jnp.tile(q_gain, 2)[None, :], jnp.tile(k_gain, 2)[None, :]
    partner = lambda g: g.reshape(1, -1, 2, ROPE_FREQS)[:, :, ::-1, :].reshape(1, LANES)
    args = (shift.reshape(1).astype(F32), p, p, p, pc, pc, pc, qg2, kg2,
            cos_tab * qg2, sin_tab * partner(qg2), cos_tab * kg2, sin_tab * partner(kg2))
    res = lax.cond(shift <= GLB_MAX_SHIFT, lambda a: call(True)(*a), lambda a: call(False)(*a), args)
    return res if ctx_out else (res[0], None)


def _win_kernel(sink_ref, q_ref, k_ref, v_ref, qc_ref, kc_ref, vc_ref,
                cos_ref, sin_ref, *rest, ctx_out):
    if ctx_out:
        y_ref, yc_ref, kd_scr, vd_scr, qa_scr = rest
    else:
        y_ref, kd_scr, vd_scr, qa_scr = rest
        yc_ref = None
    j = pl.program_id(1)
    n = q_ref.shape[1]
    nc = qc_ref.shape[1]
    tq = CHUNK
    nb = n // tq
    lat0 = nc + tq

    ones = jnp.ones((LANES, LANES), BF16)

    def sq_norms(xb, acc):
        xf = xb.astype(F32)
        return jnp.maximum(acc, _dot((xf * xf).astype(BF16), ones))

    zeros = jnp.zeros((tq, LANES), BF16)
    for r0 in (nc, lat0 + n):
        kd_scr[r0:r0 + tq, :] = zeros
        vd_scr[r0:r0 + tq, :] = zeros
    ksq_max = jnp.zeros((tq, LANES), F32)
    for t in range(nc // tq):
        rows = slice(t * tq, (t + 1) * tq)
        kd = _own_head_dup(kc_ref[0, rows, :].astype(F32), j).astype(BF16)
        kd_scr[rows, :] = kd
        ksq_max = sq_norms(kd, ksq_max)
        vd_scr[rows, :] = _values_with_ones(vc_ref[0, rows, :].astype(F32), j).astype(BF16)

    def prep(t, ksq_max):
        rows = pl.ds(pl.multiple_of(t * tq, tq), tq)
        dst = pl.ds(pl.multiple_of(lat0 + t * tq, tq), tq)
        kn = _rope_roll(k_ref[0, rows, :].astype(F32), cos_ref[rows, :], sin_ref[rows, :])
        kd = _own_head_dup(kn, j).astype(BF16)
        kd_scr[dst, :] = kd
        vd_scr[dst, :] = _values_with_ones(v_ref[0, rows, :].astype(F32), j).astype(BF16)
        return sq_norms(kd, ksq_max)

    ksq_max = lax.fori_loop(0, nb, prep, ksq_max, unroll=4)

    sk = jnp.where(_iota((2 * tq, 1), 0) < tq, sink_ref[2 * j], sink_ref[2 * j + 1])

    def stack_queries(t, qsq_max):
        rows = pl.ds(pl.multiple_of(t * tq, tq), tq)
        qn = _rope_roll(q_ref[0, rows, :].astype(F32), cos_ref[rows, :], sin_ref[rows, :])
        q2 = _stack_heads((qn * QK_SCALE).astype(BF16))
        qa_scr[t] = q2
        return sq_norms(q2, qsq_max)

    qsq_max = lax.fori_loop(0, nb, stack_queries, jnp.zeros((2 * tq, LANES), F32), unroll=4)
    bound_max = jnp.sqrt(jnp.max(qsq_max, keepdims=True) * (0.5 * jnp.max(ksq_max, keepdims=True))) * GLB_BOUND_SLACK
    shift_all = jnp.maximum(bound_max, sk)

    sum_lanes = _iota((2 * tq, LANES), 1) >= HEAD_DIM

    def softmax_pv(parts, sk, shift):
        if shift is None:
            shift = sk
            for s, _ in parts:
                shift = jnp.maximum(shift, jnp.max(s, axis=-1, keepdims=True))
        acc = jnp.where(sum_lanes, jnp.exp(sk - shift), 0.0)
        for s, vals in parts:
            acc = acc + _dot(jnp.exp(s - shift).astype(BF16), vals)
        return _normalize_heads(acc)

    band_c = _iota((2 * tq, 3 * tq), 1)
    off = band_c - _iota((2 * tq, 3 * tq), 0) % tq
    band_bias = jnp.where((off >= 0) & (off <= 2 * WINDOW), 0.0, NEG)
    col1 = _iota((1, 3 * tq), 1)
    first_bias = jnp.where(col1 < tq, NEG, 0.0)
    last_bias = jnp.where(col1 >= 2 * tq, NEG, 0.0)

    def q_tile(t, carry, bounded):
        rows = pl.ds(pl.multiple_of(t * tq, tq), tq)
        band = pl.ds(pl.multiple_of(nc + t * tq, tq), 3 * tq)
        q2 = qa_scr[t]
        s_ctx = _dot_nt(q2, kd_scr[0:nc, :])
        edge = jnp.where(t == 0, first_bias, 0.0) + jnp.where(t == nb - 1, last_bias, 0.0)
        s_band = _dot_nt(q2, kd_scr[band, :]) + (band_bias + edge)
        out = softmax_pv([(s_ctx, vd_scr[0:nc, :]), (s_band, vd_scr[band, :])], sk,
                         shift_all if bounded else None)
        y_ref[0, rows, :] = out.astype(y_ref.dtype)
        return carry

    small_logits = jnp.max(bound_max) <= GLB_MAX_SHIFT

    @pl.when(small_logits)
    def _():
        lax.fori_loop(0, nb, functools.partial(q_tile, bounded=True), 0, unroll=2)

    @pl.when(jnp.logical_not(small_logits))
    def _():
        lax.fori_loop(0, nb, functools.partial(q_tile, bounded=False), 0, unroll=2)

    if ctx_out:
        for t in range(nc // tq):
            rows = slice(t * tq, (t + 1) * tq)
            q2 = _stack_heads((qc_ref[0, rows, :].astype(F32) * QK_SCALE).astype(BF16))
            s_ctx = _dot_nt(q2, kd_scr[0:nc, :])
            yc_ref[0, rows, :] = softmax_pv([(s_ctx, vd_scr[0:nc, :])], sk, None).astype(yc_ref.dtype)


def _window_gqa(p, pc, sink, rope_tabs, ctx_out):
    b, n, _ = p.shape
    nc = pc.shape[1]
    tab = pl.BlockSpec((n, LANES), lambda i, j: (0, 0))
    kv = lambda rows, idx: pl.BlockSpec((1, rows, LANES), lambda i, j, idx=idx: (i, 0, idx))
    out_specs = [pl.BlockSpec((1, n, LANES), lambda i, j: (i, 0, j))]
    out_shape = [jax.ShapeDtypeStruct((b, n, GROUP_W), BF16)]
    if ctx_out:
        out_specs.append(pl.BlockSpec((1, nc, LANES), lambda i, j: (i, 0, j)))
        out_shape.append(jax.ShapeDtypeStruct((b, nc, GROUP_W), BF16))
    res = pl.pallas_call(
        functools.partial(_win_kernel, ctx_out=ctx_out),
        grid=(b, 2),
        in_specs=[pl.BlockSpec(memory_space=pltpu.SMEM),
                  _lane_block(n, 0), kv(n, 2), kv(n, 3), _lane_block(nc, 0), kv(nc, 2), kv(nc, 3),
                  tab, tab],
        out_specs=out_specs,
        out_shape=out_shape,
        scratch_shapes=[pltpu.VMEM((n + nc + 2 * CHUNK, LANES), BF16)] * 2
        + [pltpu.VMEM((n // CHUNK, 2 * CHUNK, LANES), BF16)],
        compiler_params=pltpu.CompilerParams(dimension_semantics=("arbitrary", "arbitrary"),
                                             vmem_limit_bytes=VMEM_LIMIT),
        name="window_gqa",
    )(sink, p, p, p, pc, pc, pc, *rope_tabs)
    return res if ctx_out else (res[0], None)


def _mid_row_index(level, rev):
    h = 1 << level
    t = np.arange(CHUNK)
    return (t // (2 * h)) * (2 * h) + (h if rev else h - 1)


def _mid_rows(b, level, rev):
    h = 1 << level
    idx = _mid_row_index(level, rev)
    pieces = [jnp.broadcast_to(b[int(idx[m]):int(idx[m]) + 1, :], (2 * h, b.shape[1])) for m in range(0, CHUNK, 2 * h)]
    return pieces[0] if len(pieces) == 1 else jnp.concatenate(pieces, axis=0)


HG_MATMUL_LEVELS = 3


def _cumsum_matrices():
    u = np.arange(CHUNK)[None, :]
    out = np.zeros((2, (1 + HG_MATMUL_LEVELS) * CHUNK, CHUNK), np.float32)
    for d, rev in enumerate((False, True)):
        rows = [np.arange(CHUNK)] + [_mid_row_index(level, rev) for level in range(HG_MATMUL_LEVELS)]
        for k, r in enumerate(rows):
            out[d, k * CHUNK:(k + 1) * CHUNK] = (u >= r[:, None]) if rev else (u <= r[:, None])
    return out


HG_GROUP = 8


def _level_masks():
    t = np.arange(2 * CHUNK)[:, None] % CHUNK
    s = np.arange(CHUNK)[None, :]
    out = np.zeros((2, N_LEVELS, 2 * CHUNK, CHUNK), np.float32)
    for level in range(N_LEVELS):
        same = (t >> (level + 1)) == (s >> (level + 1))
        t_hi, s_hi = (t >> level) & 1, (s >> level) & 1
        out[0, level] = same & (t_hi == 1) & (s_hi == 0)
        out[1, level] = same & (t_hi == 0) & (s_hi == 1)
    return out


def _hg_kernel(lvl_ref, cum_ref, lb_ref, q_ref, zf_ref, zb_ref, i_ref, g_ref,
               qc_ref, zfc_ref, zbc_ref, ic_ref, gc_ref, *rest, ctx_out):
    if ctx_out:
        y_ref, yc_ref, o_scr = rest
    else:
        y_ref, o_scr = rest
        yc_ref = None
    c_len = CHUNK
    n_lat = q_ref.shape[1] // c_len
    n_ctx = qc_ref.shape[1] // c_len
    lb = lb_ref[0]
    bd = _head_block_mask()
    bd_ones = jnp.where(bd, 1.0, 0.0).astype(BF16)
    bdm = jnp.where(bd, 1.0 / HEAD_DIM, 0.0).astype(BF16)


    def gates(z, rev):
        sg = jax.nn.sigmoid(z.astype(F32))
        f = lb + (1.0 - lb) * sg
        x = jnp.log(jnp.maximum(f, TINY))
        kk = (1.0 - lb) * (1.0 - sg)
        cum = _split_dot_lhs(cum_ref[1 if rev else 0], x)
        return kk, [cum[k * c_len:(k + 1) * c_len] for k in range(1 + HG_MATMUL_LEVELS)]

    def intra_scores(qraw, kk, cum, v, rev):
        b = cum[0]
        qs = _silu(qraw.astype(F32))
        o = _split_dot(qs * kk, bd_ones) * v.astype(F32)
        a = None
        for level in range(N_LEVELS):
            mid = cum[1 + level] if level < HG_MATMUL_LEVELS else _mid_rows(b, level, rev)
            e = jnp.exp(-jnp.abs(b - mid))
            al = _dot_nt(_stack_heads((qs * e).astype(BF16)), (kk * e).astype(BF16))
            al = al * lvl_ref[1 if rev else 0, level]
            a = al if a is None else a + al
        return qs, o, a

    def with_state(qs, o, kk, b, v, st, rev):
        edge = b[0:1, :] if rev else b[c_len - 1:c_len, :]
        if o is not None:
            o = o + _dot_nt((qs * jnp.exp(b)).astype(BF16), st.astype(BF16))
        u = _dot_tn(v, (kk * jnp.exp(edge - b)).astype(BF16))
        return o, st * jnp.exp(edge) + jnp.where(bd, u, 0.0)

    def run_group(chunks, st, rev, need_out):
        g = [gates(z, rev) for _, z, _ in chunks]
        if need_out:
            s = [intra_scores(q, kk, cum, v, rev) for (q, _, v), (kk, cum) in zip(chunks, g)]
            s = [(qs, o + _unstack_heads(_dot(a.astype(BF16), v))) for (qs, o, a), (_, _, v) in zip(s, chunks)]
        else:
            s = [(None, None)] * len(chunks)
        outs = []
        for (qs, o), (kk, cum), (_, _, v) in zip(s, g, chunks):
            o, st = with_state(qs, o, kk, cum[0], v, st, rev)
            outs.append(o)
        return outs, st

    def finish(o, g):
        return o * lax.rsqrt(_dot((o * o).astype(BF16), bdm) + EPS) * _silu(g.astype(F32))

    for rev in (False, True):
        z_ref, zc_ref = (zb_ref, zbc_ref) if rev else (zf_ref, zfc_ref)
        order = list(reversed(range(n_ctx))) if rev else list(range(n_ctx))
        ctx_rows = [slice(c * c_len, (c + 1) * c_len) for c in order]
        outs, st = run_group([(qc_ref[0, r, :], zc_ref[0, r, :], ic_ref[0, r, :]) for r in ctx_rows],
                             jnp.zeros((LANES, LANES), F32), rev, ctx_out)
        if ctx_out:
            for r, o in zip(ctx_rows, outs):
                if rev:
                    yc_ref[0, r, :] = finish(o_scr[r, :] + o, gc_ref[0, r, :]).astype(yc_ref.dtype)
                else:
                    o_scr[r, :] = o

        def lat(i, st, rev=rev, z_ref=z_ref):
            first = (n_lat - 1 - i * HG_GROUP) if rev else i * HG_GROUP
            idx = [first - k if rev else first + k for k in range(HG_GROUP)]
            rows = [pl.ds(pl.multiple_of(ci * c_len, c_len), c_len) for ci in idx]
            srows = [pl.ds(pl.multiple_of((ci + n_ctx) * c_len, c_len), c_len) for ci in idx]
            outs, st = run_group([(q_ref[0, r, :], z_ref[0, r, :], i_ref[0, r, :]) for r in rows], st, rev, True)
            for r, sr, o in zip(rows, srows, outs):
                if rev:
                    y_ref[0, r, :] = finish(o_scr[sr, :] + o, g_ref[0, r, :]).astype(y_ref.dtype)
                else:
                    o_scr[sr, :] = o
            return st

        lax.fori_loop(0, n_lat // HG_GROUP, lat, st)


def _split_dot_lhs(w, x):
    hi = x.astype(BF16)
    lo = (x - hi.astype(F32)).astype(BF16)
    return _dot(w, hi) + _dot(w, lo)


def _hgrn2(p, pc, lower_bound, ctx_out):
    b, n, _ = p.shape
    nc = pc.shape[1]
    out_specs = [pl.BlockSpec((1, n, LANES), lambda i, j: (i, 0, j))]
    out_shape = [jax.ShapeDtypeStruct((b, n, GROUP_W), BF16)]
    if ctx_out:
        out_specs.append(pl.BlockSpec((1, nc, LANES), lambda i, j: (i, 0, j)))
        out_shape.append(jax.ShapeDtypeStruct((b, nc, GROUP_W), BF16))
    res = pl.pallas_call(
        functools.partial(_hg_kernel, ctx_out=ctx_out),
        grid=(b, 2),
        in_specs=[pl.BlockSpec((2, N_LEVELS, 2 * CHUNK, CHUNK), lambda i, j: (0, 0, 0, 0)),
                  pl.BlockSpec((2, (1 + HG_MATMUL_LEVELS) * CHUNK, CHUNK), lambda i, j: (0, 0, 0)),
                  pl.BlockSpec((1, 1, LANES), lambda i, j: (j, 0, 0))]
        + [_lane_block(n, 2 * s) for s in range(5)] + [_lane_block(nc, 2 * s) for s in range(5)],
        out_specs=out_specs,
        out_shape=out_shape,
        scratch_shapes=[pltpu.VMEM((n + nc, LANES), F32)],
        compiler_params=pltpu.CompilerParams(dimension_semantics=("arbitrary", "arbitrary"),
                                             vmem_limit_bytes=VMEM_LIMIT),
        name="hgrn2",
    )(jnp.asarray(_level_masks()), jnp.asarray(_cumsum_matrices(), BF16), lower_bound.reshape(2, 1, LANES),
      p, p, p, p, p, pc, pc, pc, pc, pc)
    return res if ctx_out else (res[0], None)


FF_COLS = 1024


def _out_mlp_kernel(x_ref, yr_ref, yg_ref, yw_ref, yh_ref, mod_ref, wo_ref, wu_ref, wd_ref,
                    g1_ref, b1_ref, g2_ref, b2_ref, o_ref, h_scr):
    half = x_ref.shape[1] // 2
    n_ff = wu_ref.shape[1] // FF_COLS
    halves = (slice(0, half), slice(half, 2 * half))

    def mixer_proj(r):
        y = None
        for s, y_ref in enumerate((yr_ref, yg_ref, yw_ref, yh_ref)):
            part = _dot(y_ref[0, r, :], wo_ref[s * GROUP_W:(s + 1) * GROUP_W, :])
            y = part if y is None else y + part
        return y

    def post_mixer(r, y):
        x1 = _ln_rows(ALPHA * x_ref[0, r, :] + mod_ref[0, 2:3, :] * y) * g1_ref[...] + b1_ref[...]
        o_ref[0, r, :] = x1
        h_scr[r, :] = (_ln_rows(x1) * (1.0 + mod_ref[0, 4:5, :]) + mod_ref[0, 3:4, :]).astype(BF16)

    def up(r, f):
        return _dot(h_scr[r, :], wu_ref[:, f * FF_COLS:(f + 1) * FF_COLS])

    def down(u, f):
        u = jnp.maximum(u, 0.0)
        return _dot((u * u).astype(BF16), wd_ref[f * FF_COLS:(f + 1) * FF_COLS, :])

    def post_mlp(r, acc):
        o_ref[0, r, :] = _ln_rows(ALPHA * o_ref[0, r, :] + mod_ref[0, 5:6, :] * acc) * g2_ref[...] + b2_ref[...]

    ys = [mixer_proj(r) for r in halves]
    post_mixer(halves[0], ys[0])
    units = [(r, f) for r in halves for f in range(n_ff)]
    u_next = up(*units[0])
    post_mixer(halves[1], ys[1])
    acc = None
    for k, (r, f) in enumerate(units):
        u = u_next
        if k + 1 < len(units):
            u_next = up(*units[k + 1])
        part = down(u, f)
        acc = part if f == 0 else acc + part
        if f == n_ff - 1:
            post_mlp(r, acc)


def _out_mlp(x, ys, mod, w_out, w_up, w_down, g1, b1, g2, b2, tm, shared_mod):
    b, n, d = x.shape
    mod_map = (lambda i, t: (0, 0, 0)) if shared_mod else (lambda i, t: (i, 0, 0))
    full = lambda a: pl.BlockSpec(a.shape, lambda i, t: (0,) * a.ndim, pipeline_mode=pl.Buffered(1))
    vecs = [v.reshape(1, d) for v in (g1, b1, g2, b2)]
    return pl.pallas_call(
        _out_mlp_kernel,
        grid=(b, n // tm),
        in_specs=[pl.BlockSpec((1, tm, d), lambda i, t: (i, t, 0))]
        + [pl.BlockSpec((1, tm, GROUP_W), lambda i, t: (i, t, 0))] * 4
        + [pl.BlockSpec((1, 6, d), mod_map), full(w_out), full(w_up), full(w_down)]
        + [full(v) for v in vecs],
        out_specs=pl.BlockSpec((1, tm, d), lambda i, t: (i, t, 0)),
        out_shape=jax.ShapeDtypeStruct((b, n, d), F32),
        scratch_shapes=[pltpu.VMEM((tm, d), BF16)],
        compiler_params=pltpu.CompilerParams(dimension_semantics=("arbitrary", "arbitrary"),
                                             vmem_limit_bytes=VMEM_LIMIT),
        name="out_proj_mlp",
    )(x, *ys, mod, w_out, w_up, w_down, *vecs)


def _rope_tables(n):
    rows = n // GRID_W
    row = jnp.repeat(jnp.arange(rows), GRID_W).astype(F32)
    col = (jnp.arange(rows * GRID_W) % GRID_W).astype(F32)
    inv = ROPE_THETA ** (-jnp.arange(ROPE_FREQS, dtype=F32) / ROPE_FREQS)
    cos, sin = [], []
    for pos in (row, col):
        ang = pos[:, None] * inv
        cos += [jnp.cos(ang), jnp.cos(ang)]
        sin += [-jnp.sin(ang), jnp.sin(ang)]
    return tuple(jnp.tile(jnp.concatenate(t, axis=1), (1, LANES // HEAD_DIM)) for t in (cos, sin))


def kernel(x, c, ctx, c_ctx, w_ada, b_ada, w_in, ret_decay_logit, gqa_q_gain, gqa_k_gain,
           swa_sink, hgrn_lb, w_out, ln1_g, ln1_b, w_up, w_down, ln2_g, ln2_b):
    batch, n, d = x.shape
    depth = w_ada.shape[0]
    rope_tabs = _rope_tables(n)
    p_lb = jax.nn.softmax(hgrn_lb.astype(F32), axis=0)
    lower_bounds = jnp.cumsum(p_lb, axis=0) - p_lb[0]
    log_decay = jax.nn.log_sigmoid(ret_decay_logit.astype(F32))

    pad = (-(batch + 1)) % 8
    cvec = jnp.concatenate([c, c_ctx[None, :], jnp.zeros((pad, d), F32)], axis=0)
    mods = _modulation(cvec, w_ada, b_ada)
    w_in_bf, w_out_bf, w_up_bf, w_down_bf = (w.astype(BF16) for w in (w_in, w_out, w_up, w_down))

    xc = ctx
    for l in range(depth):
        ctx_out = l < depth - 1
        m_lat = mods[l, :batch].reshape(batch, 6, d)
        m_ctx = mods[l, batch:batch + 1].reshape(1, 6, d)
        p = _project(x, m_lat, w_in_bf[l], 1024, False)
        pc = _project(xc, m_ctx, w_in_bf[l], xc.shape[1], True)
        y_ret, yc_ret = _retention(p[0], pc[0], log_decay[l], rope_tabs, ctx_out)
        y_glb, yc_glb = _global_gqa(p[1], pc[1], gqa_q_gain[l], gqa_k_gain[l], rope_tabs, ctx_out)
        y_win, yc_win = _window_gqa(p[2], pc[2], swa_sink[l], rope_tabs, ctx_out)
        y_hg, yc_hg = _hgrn2(p[3], pc[3], lower_bounds[l], ctx_out)
        post = (w_out_bf[l], w_up_bf[l], w_down_bf[l], ln1_g[l], ln1_b[l], ln2_g[l], ln2_b[l])
        x = _out_mlp(x, (y_ret, y_glb, y_win, y_hg), m_lat, *post, 1024, False)
        if ctx_out:
            xc = _out_mlp(xc, (yc_ret, yc_glb, yc_win, yc_hg), m_ctx, *post, 256, True)
    return x
```

```python
import functools

import numpy as np
import jax
import jax.numpy as jnp
from jax import lax
from jax.experimental import pallas as pl
from jax.experimental.pallas import tpu as pltpu

F32 = jnp.float32
BF16 = jnp.bfloat16

D_MODEL = 1024
CTX_LEN = 256
GRID_W = 64
HEAD_DIM = 64
GROUP_W = 256
D_FF = 4 * D_MODEL
DEPTH = 2
ROPE_THETA = 10000.0
ROPE_FREQS = HEAD_DIM // 4
WINDOW = 128
ALPHA = (2 * DEPTH) ** 0.25
EPS = 1e-6
NEG = -1e30
TINY = 1e-30
QK_SCALE = HEAD_DIM ** -0.5

LANES = 128
CHUNK = 128
N_LEVELS = 7
VMEM_LIMIT = 56 * 1024 * 1024


def _dot(a, b):
    return jnp.dot(a, b, preferred_element_type=F32)


def _dot_nt(a, b):
    return lax.dot_general(a, b, (((1,), (1,)), ((), ())), preferred_element_type=F32)


def _dot_tn(a, b):
    return lax.dot_general(a, b, (((0,), (0,)), ((), ())), preferred_element_type=F32)


def _split_dot(x, w):
    hi = x.astype(BF16)
    lo = (x - hi.astype(F32)).astype(BF16)
    return _dot(hi, w) + _dot(lo, w)


def _silu(x):
    return x * jax.nn.sigmoid(x)


def _ln_rows(x):
    mu = jnp.mean(x, axis=-1, keepdims=True)
    xc = x - mu
    var = jnp.mean(xc * xc, axis=-1, keepdims=True)
    return xc * lax.rsqrt(var + EPS)


def _rope_roll(x, c, sn):
    first = (_iota(x.shape, 1) & ROPE_FREQS) == 0
    partner = jnp.where(first, pltpu.roll(x, LANES - ROPE_FREQS, 1), pltpu.roll(x, ROPE_FREQS, 1))
    return x * c + partner * sn


def _iota(shape, dim):
    return lax.broadcasted_iota(jnp.int32, shape, dim)


def _head_block_mask():
    return (_iota((LANES, LANES), 0) // HEAD_DIM) == (_iota((LANES, LANES), 1) // HEAD_DIM)


def _stack_heads(qb):
    lo = _iota(qb.shape, 1) < HEAD_DIM
    zero = jnp.zeros_like(qb)
    return jnp.concatenate([jnp.where(lo, qb, zero), jnp.where(lo, zero, qb)], axis=0)


def _unstack_heads(r):
    t = r.shape[0] // 2
    lo = _iota((t, LANES), 1) < HEAD_DIM
    return jnp.where(lo, r[:t], r[t:])


def _mod_kernel(c_ref, w_ref, b_ref, o_ref):
    s = _silu(c_ref[...])
    o_ref[0] = _dot(s.astype(BF16), w_ref[0].astype(BF16)) + b_ref[0]


def _modulation(cvec, w_ada, b_ada):
    rows = cvec.shape[0]
    depth, d, n = w_ada.shape
    nt = n // d
    return pl.pallas_call(
        _mod_kernel,
        grid=(depth, nt),
        in_specs=[pl.BlockSpec((rows, d), lambda l, j: (0, 0)),
                  pl.BlockSpec((1, d, d), lambda l, j: (l, 0, j)),
                  pl.BlockSpec((1, 1, d), lambda l, j: (l, 0, j))],
        out_specs=pl.BlockSpec((1, rows, d), lambda l, j: (l, 0, j)),
        out_shape=jax.ShapeDtypeStruct((depth, rows, n), F32),
        compiler_params=pltpu.CompilerParams(dimension_semantics=("arbitrary", "arbitrary"),
                                             vmem_limit_bytes=VMEM_LIMIT),
        name="modulation",
    )(cvec, w_ada, b_ada.reshape(depth, 1, n))


PROJ_GROUPS = (4 * GROUP_W, 2 * GROUP_W, 2 * GROUP_W, 5 * GROUP_W)
PROJ_COLS = 256


def _proj_kernel(x_ref, mod_ref, w_ref, o_ret, o_glb, o_win, o_hg, h_scr):
    half = x_ref.shape[1] // 2
    halves = (slice(0, half), slice(half, 2 * half))

    def modulate(r):
        h_scr[r, :] = (_ln_rows(x_ref[0, r, :]) * (1.0 + mod_ref[0, 1:2, :]) + mod_ref[0, 0:1, :]).astype(BF16)

    def project(r, first_chunk_only):
        col = 0
        for o_ref, width in zip((o_ret, o_glb, o_win, o_hg), PROJ_GROUPS):
            for j in range(width // PROJ_COLS):
                if first_chunk_only == (col == 0):
                    o_ref[0, r, j * PROJ_COLS:(j + 1) * PROJ_COLS] = _dot(
                        h_scr[r, :], w_ref[:, col:col + PROJ_COLS]).astype(BF16)
                col += PROJ_COLS

    modulate(halves[0])
    project(halves[0], True)
    modulate(halves[1])
    project(halves[0], False)
    project(halves[1], True)
    project(halves[1], False)


def _project(x, mod, w_in_bf, tm, shared_mod):
    b, n, d = x.shape
    n_in = w_in_bf.shape[1]
    mod_map = (lambda i, t: (0, 0, 0)) if shared_mod else (lambda i, t: (i, 0, 0))
    return pl.pallas_call(
        _proj_kernel,
        grid=(b, n // tm),
        in_specs=[pl.BlockSpec((1, tm, d), lambda i, t: (i, t, 0)),
                  pl.BlockSpec((1, 6, d), mod_map),
                  pl.BlockSpec((d, n_in), lambda i, t: (0, 0), pipeline_mode=pl.Buffered(1))],
        out_specs=[pl.BlockSpec((1, tm, w), lambda i, t: (i, t, 0)) for w in PROJ_GROUPS],
        out_shape=[jax.ShapeDtypeStruct((b, n, w), BF16) for w in PROJ_GROUPS],
        scratch_shapes=[pltpu.VMEM((tm, d), BF16)],
        compiler_params=pltpu.CompilerParams(dimension_semantics=("arbitrary", "arbitrary"),
                                             vmem_limit_bytes=VMEM_LIMIT),
        name="adaln_in_proj",
    )(x, mod, w_in_bf)


RET_GROUP = 8


def _ret_kernel(lg_ref, q_ref, k_ref, v_ref, g_ref, qc_ref, kc_ref, vc_ref, gc_ref,
                cos_ref, sin_ref, *rest, ctx_out):
    if ctx_out:
        y_ref, yc_ref, o_scr, qs_scr, ks_scr = rest
    else:
        y_ref, o_scr, qs_scr, ks_scr = rest
        yc_ref = None
    j = pl.program_id(1)
    c_len = CHUNK
    n_lat = q_ref.shape[1] // c_len
    n_ctx = qc_ref.shape[1] // c_len

    hi1 = _iota((1, LANES), 1) >= HEAD_DIM
    rowi = _iota((c_len, LANES), 0)
    coli = _iota((c_len, LANES), 1)
    pos = rowi.astype(F32)
    lgf = jnp.where(hi1, lg_ref[0, 2 * j + 1], lg_ref[0, 2 * j])
    lgb = jnp.where(hi1, lg_ref[1, 2 * j + 1], lg_ref[1, 2 * j])
    wqf = jnp.exp(lgf * (pos + 1.0))
    wqb = jnp.exp(lgb * (c_len - pos))
    wkf = jnp.exp(lgf * (c_len - 1.0 - pos))
    wkb = jnp.exp(lgb * pos)
    dcf = jnp.exp(lgf * c_len)
    dcb = jnp.exp(lgb * c_len)
    dlt = (rowi - coli).astype(F32)

    def head_mask(e):
        lf = lg_ref[0, 2 * j + e]
        lb = lg_ref[1, 2 * j + e]
        return (jnp.where(dlt >= 0, jnp.exp(lf * jnp.maximum(dlt, 0.0)), 0.0)
                + jnp.where(dlt <= 0, jnp.exp(lb * jnp.maximum(-dlt, 0.0)), 0.0))

    m2 = jnp.concatenate([head_mask(0), head_mask(1)], axis=0)
    bd = _head_block_mask()
    bdm = jnp.where(bd, 1.0 / HEAD_DIM, 0.0).astype(BF16)


    def state_chain(st, us, decay):
        seen = []
        for u in us:
            seen.append(st)
            st = st * decay + jnp.where(bd, u, 0.0)
        return seen, st

    def fwd_group(cs, qkv, st, need_out):
        k8s = [k * QK_SCALE for _, k, _ in qkv]
        us = [_dot_tn(v, (k8 * wkf).astype(BF16)) for (_, _, v), k8 in zip(qkv, k8s)]
        seen, st = state_chain(st, us, dcf)
        if need_out:
            a2s = [_dot_nt(_stack_heads(q.astype(BF16)), k8.astype(BF16)) for (q, _, _), k8 in zip(qkv, k8s)]
            rs = [_dot((a2 * m2).astype(BF16), v) for a2, (_, _, v) in zip(a2s, qkv)]
            inter = [_dot_nt((q * wqf).astype(BF16), s.astype(BF16)) for (q, _, _), s in zip(qkv, seen)]
            for c, r, x in zip(cs, rs, inter):
                o_scr[pl.ds(c * c_len, c_len), :] = _unstack_heads(r) + x
        for c, (q, _, _), k8 in zip(cs, qkv, k8s):
            qs_scr[pl.ds(c * c_len, c_len), :] = q
            ks_scr[pl.ds(c * c_len, c_len), :] = k8
        return st

    def bwd_group(cs, vgs, st, out_ref, row0s, need_out):
        qs = [qs_scr[pl.ds(c * c_len, c_len), :] for c in cs]
        k8s = [ks_scr[pl.ds(c * c_len, c_len), :] for c in cs]
        us = [_dot_tn(v, (k8 * wkb).astype(BF16)) for (v, _), k8 in zip(vgs, k8s)]
        seen, st = state_chain(st, us, dcb)
        if need_out:
            os_ = [o_scr[pl.ds(c * c_len, c_len), :] + _dot_nt((q * wqb).astype(BF16), s.astype(BF16))
                   for c, q, s in zip(cs, qs, seen)]
            xcs = [o - _split_dot(o, bdm) for o in os_]
            vars_ = [_dot((xc * xc).astype(BF16), bdm) for xc in xcs]
            for row0, xc, var, (_, g) in zip(row0s, xcs, vars_, vgs):
                out_ref[0, pl.ds(row0, c_len), :] = (xc * lax.rsqrt(var + EPS) * _silu(g)).astype(out_ref.dtype)
        return st

    ctx_rows = [slice(c * c_len, (c + 1) * c_len) for c in range(n_ctx)]
    st = fwd_group(list(range(n_ctx)),
                   [(qc_ref[0, r, :].astype(F32), kc_ref[0, r, :].astype(F32), vc_ref[0, r, :]) for r in ctx_rows],
                   jnp.zeros((LANES, LANES), F32), ctx_out)

    def lat_fwd(i, st):
        idx = [i * RET_GROUP + k for k in range(RET_GROUP)]
        qkv = []
        for ci in idx:
            rows = pl.ds(pl.multiple_of(ci * c_len, c_len), c_len)
            cs, sn = cos_ref[rows, :], sin_ref[rows, :]
            qkv.append((_rope_roll(q_ref[0, rows, :].astype(F32), cs, sn),
                        _rope_roll(k_ref[0, rows, :].astype(F32), cs, sn), v_ref[0, rows, :]))
        return fwd_group([ci + n_ctx for ci in idx], qkv, st, True)

    lax.fori_loop(0, n_lat // RET_GROUP, lat_fwd, st)

    st = bwd_group(list(reversed(range(n_ctx))),
                   [(vc_ref[0, r, :], gc_ref[0, r, :].astype(F32) if ctx_out else None) for r in reversed(ctx_rows)],
                   jnp.zeros((LANES, LANES), F32), yc_ref, [c * c_len for c in reversed(range(n_ctx))], ctx_out)

    def lat_bwd(i, st):
        idx = [n_lat - 1 - i * RET_GROUP - k for k in range(RET_GROUP)]
        row0s = [pl.multiple_of(ci * c_len, c_len) for ci in idx]
        vgs = [(v_ref[0, pl.ds(r0, c_len), :], g_ref[0, pl.ds(r0, c_len), :].astype(F32)) for r0 in row0s]
        return bwd_group([ci + n_ctx for ci in idx], vgs, st, y_ref, row0s, True)

    lax.fori_loop(0, n_lat // RET_GROUP, lat_bwd, st)


def _lane_block(n, idx):
    return pl.BlockSpec((1, n, LANES), lambda i, j, idx=idx: (i, 0, idx + j))


def _retention(p, pc, log_decay, rope_tabs, ctx_out):
    b, n, _ = p.shape
    nc = pc.shape[1]
    tab = pl.BlockSpec((n, LANES), lambda i, j: (0, 0))
    out_specs = [pl.BlockSpec((1, n, LANES), lambda i, j: (i, 0, j))]
    out_shape = [jax.ShapeDtypeStruct((b, n, GROUP_W), BF16)]
    if ctx_out:
        out_specs.append(pl.BlockSpec((1, nc, LANES), lambda i, j: (i, 0, j)))
        out_shape.append(jax.ShapeDtypeStruct((b, nc, GROUP_W), BF16))
    res = pl.pallas_call(
        functools.partial(_ret_kernel, ctx_out=ctx_out),
        grid=(b, 2),
        in_specs=[pl.BlockSpec(memory_space=pltpu.SMEM)]
        + [_lane_block(n, 2 * s) for s in range(4)] + [_lane_block(nc, 2 * s) for s in range(4)]
        + [tab, tab],
        out_specs=out_specs,
        out_shape=out_shape,
        scratch_shapes=[pltpu.VMEM((n + nc, LANES), F32)] * 3,
        compiler_params=pltpu.CompilerParams(dimension_semantics=("arbitrary", "arbitrary"),
                                             vmem_limit_bytes=VMEM_LIMIT),
        name="retention",
    )(log_decay, p, p, p, p, pc, pc, pc, pc, *rope_tabs)
    return res if ctx_out else (res[0], None)


def _own_head_dup(x, j):
    own = (_iota(x.shape, 1) // HEAD_DIM) == j
    return jnp.where(own, x, pltpu.roll(x, HEAD_DIM, 1))


def _values_with_ones(v, j):
    lo = _iota(v.shape, 1) < HEAD_DIM
    return jnp.where(lo, _own_head_dup(v, j), 1.0)


def _normalize_heads(r):
    t = r.shape[0] // 2
    top, bot = r[:t], r[t:]
    lo = _iota((t, LANES), 1) < HEAD_DIM
    return jnp.where(lo, top / pltpu.roll(top, HEAD_DIM, 1), pltpu.roll(bot, HEAD_DIM, 1) / bot)


GLB_KEYS = 768
GLB_TILE = 256
GLB_BOUND_SLACK = 1.01
GLB_MAX_SHIFT = 40.0


def _rms_heads(x, bdm, gain):
    return x * lax.rsqrt(_split_dot(x * x, bdm) + EPS) * gain


def _glb_kernel(shift_ref, q_ref, k_ref, v_ref, qc_ref, kc_ref, vc_ref, qg_ref, kg_ref,
                qcos_ref, qsin_ref, kcos_ref, ksin_ref, *rest, ctx_out, bounded):
    n = q_ref.shape[1]
    nc = qc_ref.shape[1]
    if ctx_out:
        y_ref, yc_ref, kd_scr, vd_scr = rest[:4]
    else:
        y_ref, kd_scr, vd_scr = rest[:3]
        yc_ref = None
    if bounded:
        qa_scr, = rest[-1:]
    else:
        q2_scr, s_scr = rest[-2:]
    j = pl.program_id(1)
    tq = CHUNK
    bdm = jnp.where(_head_block_mask(), 1.0 / HEAD_DIM, 0.0).astype(BF16)
    qg = qg_ref[...]
    kg = kg_ref[...]
    def normed_rope(xb, cos_tab, sin_tab):
        x = xb.astype(F32)
        return lax.rsqrt(_split_dot(x * x, bdm) + EPS) * _rope_roll(x, cos_tab, sin_tab)

    for t in range(nc // tq):
        rows = slice(t * tq, (t + 1) * tq)
        kn = _rms_heads(kc_ref[0, rows, :].astype(F32), bdm, kg)
        kd_scr[rows, :] = _own_head_dup(kn, j).astype(BF16)
        vd_scr[rows, :] = _values_with_ones(vc_ref[0, rows, :].astype(F32), j).astype(BF16)

    def prep(t, carry):
        rows = pl.ds(pl.multiple_of(t * tq, tq), tq)
        dst = pl.ds(pl.multiple_of(nc + t * tq, tq), tq)
        x = k_ref[0, rows, :].astype(F32)
        kn = lax.rsqrt(_split_dot(x * x, bdm) + EPS) * _rope_roll(x, kcos_ref[rows, :], ksin_ref[rows, :])
        kd_scr[dst, :] = _own_head_dup(kn, j).astype(BF16)
        vd_scr[dst, :] = _values_with_ones(v_ref[0, rows, :].astype(F32), j).astype(BF16)
        return carry

    lax.fori_loop(0, n // tq, prep, 0, unroll=4)

    n_tiles = n // tq

    if bounded:
        shift = shift_ref[0]

        def stack_queries(t, carry):
            rows = pl.ds(pl.multiple_of(t * tq, tq), tq)
            qn = normed_rope(q_ref[0, rows, :], qcos_ref[rows, :], qsin_ref[rows, :])
            qa_scr[t] = _stack_heads((qn * QK_SCALE).astype(BF16))
            return carry

        lax.fori_loop(0, n_tiles, stack_queries, 0, unroll=4)

        def bounded_tile(t, carry):
            rows = pl.ds(pl.multiple_of(t * tq, tq), tq)
            q2 = qa_scr[t]
            acc = None
            s = _dot_nt(q2, kd_scr[0:GLB_TILE, :])
            for k0 in range(0, n + nc, GLB_TILE):
                if k0 + GLB_TILE < n + nc:
                    s_next = _dot_nt(q2, kd_scr[k0 + GLB_TILE:k0 + 2 * GLB_TILE, :])
                pv = _dot(jnp.exp(s - shift).astype(BF16), vd_scr[k0:k0 + GLB_TILE, :])
                acc = pv if acc is None else acc + pv
                s = s_next
            y_ref[0, rows, :] = _normalize_heads(acc).astype(y_ref.dtype)
            return carry

        lax.fori_loop(0, n_tiles, bounded_tile, 0, unroll=4)

    n_chunks = (n + nc) // GLB_KEYS

    def scores(q2, c):
        return _dot_nt(q2, kd_scr[c * GLB_KEYS:(c + 1) * GLB_KEYS, :])

    def stacked_q(t):
        rows = pl.ds(pl.multiple_of(t * tq, tq), tq)
        qn = normed_rope(q_ref[0, rows, :], qcos_ref[rows, :], qsin_ref[rows, :])
        return _stack_heads((qn * QK_SCALE).astype(BF16))

    def online_step(s, c, m, r):
        m_c = jnp.max(s, axis=-1, keepdims=True)
        m_new = m_c if m is None else jnp.maximum(m, m_c)
        pv = _dot(jnp.exp(s - m_new).astype(BF16), vd_scr[c * GLB_KEYS:(c + 1) * GLB_KEYS, :])
        return m_new, (pv if m is None else r * jnp.exp(m - m_new) + pv)

    def q_tile(t, carry):
        q2 = q2_scr[...]
        s = s_scr[...]
        m = r = None
        for c in range(n_chunks):
            if c + 1 < n_chunks:
                s_next = scores(q2, c + 1)
            else:
                q2_next = stacked_q(jnp.minimum(t + 1, n_tiles - 1))
                q2_scr[...] = q2_next
                s_next = scores(q2_next, 0)
            m, r = online_step(s, c, m, r)
            s = s_next
        s_scr[...] = s
        y_ref[0, pl.ds(pl.multiple_of(t * tq, tq), tq), :] = _normalize_heads(r).astype(y_ref.dtype)
        return carry

    if not bounded:
        q2_scr[...] = stacked_q(0)
        s_scr[...] = scores(q2_scr[...], 0)
        lax.fori_loop(0, n_tiles, q_tile, 0)

    if ctx_out:
        for t in range(nc // tq):
            rows = slice(t * tq, (t + 1) * tq)
            qn = _rms_heads(qc_ref[0, rows, :].astype(F32), bdm, qg)
            s = _dot_nt(_stack_heads((qn * QK_SCALE).astype(BF16)), kd_scr[0:nc, :])
            e = jnp.exp(s - jnp.max(s, axis=-1, keepdims=True))
            yc_ref[0, rows, :] = _normalize_heads(_dot(e.astype(BF16), vd_scr[0:nc, :])).astype(yc_ref.dtype)


def _global_gqa(p, pc, q_gain, k_gain, rope_tabs, ctx_out):
    b, n, _ = p.shape
    nc = pc.shape[1]
    tab = pl.BlockSpec((n, LANES), lambda i, j: (0, 0))
    gain = pl.BlockSpec((1, LANES), lambda i, j: (0, 0))
    kv = lambda rows, idx: pl.BlockSpec((1, rows, LANES), lambda i, j, idx=idx: (i, 0, idx))
    out_specs = [pl.BlockSpec((1, n, LANES), lambda i, j: (i, 0, j))]
    out_shape = [jax.ShapeDtypeStruct((b, n, GROUP_W), BF16)]
    if ctx_out:
        out_specs.append(pl.BlockSpec((1, nc, LANES), lambda i, j: (i, 0, j)))
        out_shape.append(jax.ShapeDtypeStruct((b, nc, GROUP_W), BF16))
    def call(bounded):
        return pl.pallas_call(
            functools.partial(_glb_kernel, ctx_out=ctx_out, bounded=bounded),
            grid=(b, 2),
            in_specs=[pl.BlockSpec(memory_space=pltpu.SMEM),
                      _lane_block(n, 0), kv(n, 2), kv(n, 3), _lane_block(nc, 0), kv(nc, 2), kv(nc, 3),
                      gain, gain, tab, tab, tab, tab],
            out_specs=out_specs,
            out_shape=out_shape,
            scratch_shapes=[pltpu.VMEM((n + nc, LANES), BF16)] * 2
            + ([pltpu.VMEM((n // CHUNK, 2 * CHUNK, LANES), BF16)] if bounded else
               [pltpu.VMEM((2 * CHUNK, LANES), BF16), pltpu.VMEM((2 * CHUNK, GLB_KEYS), F32)]),
            compiler_params=pltpu.CompilerParams(dimension_semantics=("arbitrary", "arbitrary"),
                                                 vmem_limit_bytes=VMEM_LIMIT),
            name="global_gqa" if bounded else "global_gqa_exact_max",
        )

    shift = (HEAD_DIM * QK_SCALE * GLB_BOUND_SLACK) * jnp.max(jnp.abs(q_gain)) * jnp.max(jnp.abs(k_gain))
    cos_tab, sin_tab = rope_tabs
    qg2, kg2 = jnp.tile(q_gain, 2)[None, :], jnp.tile(k_gain, 2)[None, :]
    partner = lambda g: g.reshape(1, -1, 2, ROPE_FREQS)[:, :, ::-1, :].reshape(1, LANES)
    args = (shift.reshape(1).astype(F32), p, p, p, pc, pc, pc, qg2, kg2,
            cos_tab * qg2, sin_tab * partner(qg2), cos_tab * kg2, sin_tab * partner(kg2))
    res = lax.cond(shift <= GLB_MAX_SHIFT, lambda a: call(True)(*a), lambda a: call(False)(*a), args)
    return res if ctx_out else (res[0], None)


WIN_GROUP = 2


def _win_kernel(sink_ref, q_ref, k_ref, v_ref, qc_ref, kc_ref, vc_ref,
                cos_ref, sin_ref, *rest, ctx_out):
    if ctx_out:
        y_ref, yc_ref, kd_scr, vd_scr, qa_scr = rest
    else:
        y_ref, kd_scr, vd_scr, qa_scr = rest
        yc_ref = None
    j = pl.program_id(1)
    n = q_ref.shape[1]
    nc = qc_ref.shape[1]
    tq = CHUNK
    nb = n // tq
    lat0 = nc + tq

    ones = jnp.ones((LANES, LANES), BF16)

    def sq_norms(xb, acc):
        xf = xb.astype(F32)
        return jnp.maximum(acc, _dot((xf * xf).astype(BF16), ones))

    zeros = jnp.zeros((tq, LANES), BF16)
    for r0 in (nc, lat0 + n):
        kd_scr[r0:r0 + tq, :] = zeros
        vd_scr[r0:r0 + tq, :] = zeros
    ksq_max = jnp.zeros((tq, LANES), F32)
    for t in range(nc // tq):
        rows = slice(t * tq, (t + 1) * tq)
        kd = _own_head_dup(kc_ref[0, rows, :].astype(F32), j).astype(BF16)
        kd_scr[rows, :] = kd
        ksq_max = sq_norms(kd, ksq_max)
        vd_scr[rows, :] = _values_with_ones(vc_ref[0, rows, :].astype(F32), j).astype(BF16)

    def prep(t, ksq_max):
        rows = pl.ds(pl.multiple_of(t * tq, tq), tq)
        dst = pl.ds(pl.multiple_of(lat0 + t * tq, tq), tq)
        kn = _rope_roll(k_ref[0, rows, :].astype(F32), cos_ref[rows, :], sin_ref[rows, :])
        kd = _own_head_dup(kn, j).astype(BF16)
        kd_scr[dst, :] = kd
        vd_scr[dst, :] = _values_with_ones(v_ref[0, rows, :].astype(F32), j).astype(BF16)
        return sq_norms(kd, ksq_max)

    ksq_max = lax.fori_loop(0, nb, prep, ksq_max, unroll=4)

    sk = jnp.where(_iota((2 * tq, 1), 0) < tq, sink_ref[2 * j], sink_ref[2 * j + 1])

    def stack_queries(t, qsq_max):
        rows = pl.ds(pl.multiple_of(t * tq, tq), tq)
        qn = _rope_roll(q_ref[0, rows, :].astype(F32), cos_ref[rows, :], sin_ref[rows, :])
        q2 = _stack_heads((qn * QK_SCALE).astype(BF16))
        qa_scr[t] = q2
        return sq_norms(q2, qsq_max)

    qsq_max = lax.fori_loop(0, nb, stack_queries, jnp.zeros((2 * tq, LANES), F32), unroll=4)
    bound_max = jnp.sqrt(jnp.max(qsq_max, keepdims=True) * (0.5 * jnp.max(ksq_max, keepdims=True))) * GLB_BOUND_SLACK
    shift_all = jnp.maximum(bound_max, sk)

    sum_lanes = _iota((2 * tq, LANES), 1) >= HEAD_DIM

    def softmax_pv(parts, sk, shift):
        if shift is None:
            shift = sk
            for s, _ in parts:
                shift = jnp.maximum(shift, jnp.max(s, axis=-1, keepdims=True))
        acc = jnp.where(sum_lanes, jnp.exp(sk - shift), 0.0)
        for s, vals in parts:
            acc = acc + _dot(jnp.exp(s - shift).astype(BF16), vals)
        return _normalize_heads(acc)

    band_c = _iota((2 * tq, 3 * tq), 1)
    off = band_c - _iota((2 * tq, 3 * tq), 0) % tq
    band_bias = jnp.where((off >= 0) & (off <= 2 * WINDOW), 0.0, NEG)
    bias = {"inner": band_bias,
            "first": jnp.where(band_c < tq, NEG, band_bias),
            "last": jnp.where(band_c >= 2 * tq, NEG, band_bias)}

    def tile_group(tiles, bounded):
        def aligned(start):
            return start if isinstance(start, int) else pl.multiple_of(start, tq)

        q2s = [qa_scr[t] for t, _ in tiles]
        bands = [pl.ds(aligned(nc + t * tq), 3 * tq) for t, _ in tiles]
        s_ctx = [_dot_nt(q2, kd_scr[0:nc, :]) for q2 in q2s]
        s_band = [_dot_nt(q2, kd_scr[band, :]) + bias[kind] for q2, band, (_, kind) in zip(q2s, bands, tiles)]
        for (t, _), band, sc, sb in zip(tiles, bands, s_ctx, s_band):
            out = softmax_pv([(sc, vd_scr[0:nc, :]), (sb, vd_scr[band, :])], sk, shift_all if bounded else None)
            y_ref[0, pl.ds(aligned(t * tq), tq), :] = out.astype(y_ref.dtype)

    def all_tiles(bounded):
        tile_group([(0, "first"), (nb - 1, "last")], bounded)

        def inner(i, carry):
            tile_group([(1 + WIN_GROUP * i + k, "inner") for k in range(WIN_GROUP)], bounded)
            return carry

        lax.fori_loop(0, (nb - 2) // WIN_GROUP, inner, 0)

    small_logits = jnp.max(bound_max) <= GLB_MAX_SHIFT

    @pl.when(small_logits)
    def _():
        all_tiles(True)

    @pl.when(jnp.logical_not(small_logits))
    def _():
        all_tiles(False)

    if ctx_out:
        for t in range(nc // tq):
            rows = slice(t * tq, (t + 1) * tq)
            q2 = _stack_heads((qc_ref[0, rows, :].astype(F32) * QK_SCALE).astype(BF16))
            s_ctx = _dot_nt(q2, kd_scr[0:nc, :])
            yc_ref[0, rows, :] = softmax_pv([(s_ctx, vd_scr[0:nc, :])], sk, None).astype(yc_ref.dtype)


def _window_gqa(p, pc, sink, rope_tabs, ctx_out):
    b, n, _ = p.shape
    nc = pc.shape[1]
    tab = pl.BlockSpec((n, LANES), lambda i, j: (0, 0))
    kv = lambda rows, idx: pl.BlockSpec((1, rows, LANES), lambda i, j, idx=idx: (i, 0, idx))
    out_specs = [pl.BlockSpec((1, n, LANES), lambda i, j: (i, 0, j))]
    out_shape = [jax.ShapeDtypeStruct((b, n, GROUP_W), BF16)]
    if ctx_out:
        out_specs.append(pl.BlockSpec((1, nc, LANES), lambda i, j: (i, 0, j)))
        out_shape.append(jax.ShapeDtypeStruct((b, nc, GROUP_W), BF16))
    res = pl.pallas_call(
        functools.partial(_win_kernel, ctx_out=ctx_out),
        grid=(b, 2),
        in_specs=[pl.BlockSpec(memory_space=pltpu.SMEM),
                  _lane_block(n, 0), kv(n, 2), kv(n, 3), _lane_block(nc, 0), kv(nc, 2), kv(nc, 3),
                  tab, tab],
        out_specs=out_specs,
        out_shape=out_shape,
        scratch_shapes=[pltpu.VMEM((n + nc + 2 * CHUNK, LANES), BF16)] * 2
        + [pltpu.VMEM((n // CHUNK, 2 * CHUNK, LANES), BF16)],
        compiler_params=pltpu.CompilerParams(dimension_semantics=("arbitrary", "arbitrary"),
                                             vmem_limit_bytes=VMEM_LIMIT),
        name="window_gqa",
    )(sink, p, p, p, pc, pc, pc, *rope_tabs)
    return res if ctx_out else (res[0], None)


def _mid_row_index(level, rev):
    h = 1 << level
    t = np.arange(CHUNK)
    return (t // (2 * h)) * (2 * h) + (h if rev else h - 1)


def _mid_rows(b, level, rev):
    h = 1 << level
    idx = _mid_row_index(level, rev)
    pieces = [jnp.broadcast_to(b[int(idx[m]):int(idx[m]) + 1, :], (2 * h, b.shape[1])) for m in range(0, CHUNK, 2 * h)]
    return pieces[0] if len(pieces) == 1 else jnp.concatenate(pieces, axis=0)


HG_MATMUL_LEVELS = 3


def _cumsum_matrices():
    u = np.arange(CHUNK)[None, :]
    out = np.zeros((2, (1 + HG_MATMUL_LEVELS) * CHUNK, CHUNK), np.float32)
    for d, rev in enumerate((False, True)):
        rows = [np.arange(CHUNK)] + [_mid_row_index(level, rev) for level in range(HG_MATMUL_LEVELS)]
        for k, r in enumerate(rows):
            out[d, k * CHUNK:(k + 1) * CHUNK] = (u >= r[:, None]) if rev else (u <= r[:, None])
    return out


HG_GROUP = 8


def _level_masks():
    t = np.arange(2 * CHUNK)[:, None] % CHUNK
    s = np.arange(CHUNK)[None, :]
    out = np.zeros((2, N_LEVELS, 2 * CHUNK, CHUNK), np.float32)
    for level in range(N_LEVELS):
        same = (t >> (level + 1)) == (s >> (level + 1))
        t_hi, s_hi = (t >> level) & 1, (s >> level) & 1
        out[0, level] = same & (t_hi == 1) & (s_hi == 0)
        out[1, level] = same & (t_hi == 0) & (s_hi == 1)
    return out


def _hg_kernel(lvl_ref, cum_ref, lb_ref, q_ref, zf_ref, zb_ref, i_ref, g_ref,
               qc_ref, zfc_ref, zbc_ref, ic_ref, gc_ref, *rest, ctx_out):
    if ctx_out:
        y_ref, yc_ref, o_scr = rest
    else:
        y_ref, o_scr = rest
        yc_ref = None
    c_len = CHUNK
    n_lat = q_ref.shape[1] // c_len
    n_ctx = qc_ref.shape[1] // c_len
    lb = lb_ref[0]
    bd = _head_block_mask()
    bd_ones = jnp.where(bd, 1.0, 0.0).astype(BF16)
    bdm = jnp.where(bd, 1.0 / HEAD_DIM, 0.0).astype(BF16)


    def gates(z, rev):
        sg = jax.nn.sigmoid(z.astype(F32))
        f = lb + (1.0 - lb) * sg
        x = jnp.log(jnp.maximum(f, TINY))
        kk = (1.0 - lb) * (1.0 - sg)
        cum = _split_dot_lhs(cum_ref[1 if rev else 0], x)
        return kk, [cum[k * c_len:(k + 1) * c_len] for k in range(1 + HG_MATMUL_LEVELS)]

    def intra_scores(qraw, kk, cum, v, rev):
        b = cum[0]
        qs = _silu(qraw.astype(F32))
        o = _split_dot(qs * kk, bd_ones) * v.astype(F32)
        qs_b, kk_b = qs.astype(BF16), kk.astype(BF16)
        a = None
        for level in range(N_LEVELS):
            mid = cum[1 + level] if level < HG_MATMUL_LEVELS else _mid_rows(b, level, rev)
            e = jnp.exp(-jnp.abs(b - mid)).astype(BF16)
            al = _dot_nt(_stack_heads(qs_b * e), kk_b * e).astype(BF16) * lvl_ref[1 if rev else 0, level]
            a = al if a is None else a + al
        return qs, o, a

    def with_state(qs, o, kk, b, v, st, rev):
        edge = b[0:1, :] if rev else b[c_len - 1:c_len, :]
        if o is not None:
            o = o + _dot_nt((qs * jnp.exp(b)).astype(BF16), st.astype(BF16))
        u = _dot_tn(v, (kk * jnp.exp(edge - b)).astype(BF16))
        return o, st * jnp.exp(edge) + jnp.where(bd, u, 0.0)

    def run_group(chunks, st, rev, need_out):
        g = [gates(z, rev) for _, z, _ in chunks]
        if need_out:
            s = [intra_scores(q, kk, cum, v, rev) for (q, _, v), (kk, cum) in zip(chunks, g)]
            s = [(qs, o + _unstack_heads(_dot(a, v))) for (qs, o, a), (_, _, v) in zip(s, chunks)]
        else:
            s = [(None, None)] * len(chunks)
        outs = []
        for (qs, o), (kk, cum), (_, _, v) in zip(s, g, chunks):
            o, st = with_state(qs, o, kk, cum[0], v, st, rev)
            outs.append(o)
        return outs, st

    def finish(o, g):
        return o * lax.rsqrt(_dot((o * o).astype(BF16), bdm) + EPS) * _silu(g.astype(F32))

    for rev in (False, True):
        z_ref, zc_ref = (zb_ref, zbc_ref) if rev else (zf_ref, zfc_ref)
        order = list(reversed(range(n_ctx))) if rev else list(range(n_ctx))
        ctx_rows = [slice(c * c_len, (c + 1) * c_len) for c in order]
        outs, st = run_group([(qc_ref[0, r, :], zc_ref[0, r, :], ic_ref[0, r, :]) for r in ctx_rows],
                             jnp.zeros((LANES, LANES), F32), rev, ctx_out)
        if ctx_out:
            for r, o in zip(ctx_rows, outs):
                if rev:
                    yc_ref[0, r, :] = finish(o_scr[r, :] + o, gc_ref[0, r, :]).astype(yc_ref.dtype)
                else:
                    o_scr[r, :] = o

        def lat(i, st, rev=rev, z_ref=z_ref):
            first = (n_lat - 1 - i * HG_GROUP) if rev else i * HG_GROUP
            idx = [first - k if rev else first + k for k in range(HG_GROUP)]
            rows = [pl.ds(pl.multiple_of(ci * c_len, c_len), c_len) for ci in idx]
            srows = [pl.ds(pl.multiple_of((ci + n_ctx) * c_len, c_len), c_len) for ci in idx]
            outs, st = run_group([(q_ref[0, r, :], z_ref[0, r, :], i_ref[0, r, :]) for r in rows], st, rev, True)
            for r, sr, o in zip(rows, srows, outs):
                if rev:
                    y_ref[0, r, :] = finish(o_scr[sr, :] + o, g_ref[0, r, :]).astype(y_ref.dtype)
                else:
                    o_scr[sr, :] = o
            return st

        lax.fori_loop(0, n_lat // HG_GROUP, lat, st)


def _split_dot_lhs(w, x):
    hi = x.astype(BF16)
    lo = (x - hi.astype(F32)).astype(BF16)
    return _dot(w, hi) + _dot(w, lo)


def _hgrn2(p, pc, lower_bound, ctx_out):
    b, n, _ = p.shape
    nc = pc.shape[1]
    out_specs = [pl.BlockSpec((1, n, LANES), lambda i, j: (i, 0, j))]
    out_shape = [jax.ShapeDtypeStruct((b, n, GROUP_W), BF16)]
    if ctx_out:
        out_specs.append(pl.BlockSpec((1, nc, LANES), lambda i, j: (i, 0, j)))
        out_shape.append(jax.ShapeDtypeStruct((b, nc, GROUP_W), BF16))
    res = pl.pallas_call(
        functools.partial(_hg_kernel, ctx_out=ctx_out),
        grid=(b, 2),
        in_specs=[pl.BlockSpec((2, N_LEVELS, 2 * CHUNK, CHUNK), lambda i, j: (0, 0, 0, 0)),
                  pl.BlockSpec((2, (1 + HG_MATMUL_LEVELS) * CHUNK, CHUNK), lambda i, j: (0, 0, 0)),
                  pl.BlockSpec((1, 1, LANES), lambda i, j: (j, 0, 0))]
        + [_lane_block(n, 2 * s) for s in range(5)] + [_lane_block(nc, 2 * s) for s in range(5)],
        out_specs=out_specs,
        out_shape=out_shape,
        scratch_shapes=[pltpu.VMEM((n + nc, LANES), F32)],
        compiler_params=pltpu.CompilerParams(dimension_semantics=("arbitrary", "arbitrary"),
                                             vmem_limit_bytes=VMEM_LIMIT),
        name="hgrn2",
    )(jnp.asarray(_level_masks(), BF16), jnp.asarray(_cumsum_matrices(), BF16), lower_bound.reshape(2, 1, LANES),
      p, p, p, p, p, pc, pc, pc, pc, pc)
    return res if ctx_out else (res[0], None)


FF_COLS = 1024


def _out_mlp_kernel(x_ref, yr_ref, yg_ref, yw_ref, yh_ref, mod_ref, wo_ref, wu_ref, wd_ref,
                    g1_ref, b1_ref, g2_ref, b2_ref, o_ref, h_scr):
    half = x_ref.shape[1] // 2
    n_ff = wu_ref.shape[1] // FF_COLS
    halves = (slice(0, half), slice(half, 2 * half))

    def mixer_proj(r):
        y = None
        for s, y_ref in enumerate((yr_ref, yg_ref, yw_ref, yh_ref)):
            part = _dot(y_ref[0, r, :], wo_ref[s * GROUP_W:(s + 1) * GROUP_W, :])
            y = part if y is None else y + part
        return y

    def post_mixer(r, y):
        x1 = _ln_rows(ALPHA * x_ref[0, r, :] + mod_ref[0, 2:3, :] * y) * g1_ref[...] + b1_ref[...]
        o_ref[0, r, :] = x1
        h_scr[r, :] = (_ln_rows(x1) * (1.0 + mod_ref[0, 4:5, :]) + mod_ref[0, 3:4, :]).astype(BF16)

    def up(r, f):
        return _dot(h_scr[r, :], wu_ref[:, f * FF_COLS:(f + 1) * FF_COLS])

    def down(u, f):
        u = jnp.maximum(u, 0.0)
        return _dot((u * u).astype(BF16), wd_ref[f * FF_COLS:(f + 1) * FF_COLS, :])

    def post_mlp(r, acc):
        o_ref[0, r, :] = _ln_rows(ALPHA * o_ref[0, r, :] + mod_ref[0, 5:6, :] * acc) * g2_ref[...] + b2_ref[...]

    ys = [mixer_proj(r) for r in halves]
    post_mixer(halves[0], ys[0])
    units = [(r, f) for r in halves for f in range(n_ff)]
    u_next = up(*units[0])
    post_mixer(halves[1], ys[1])
    acc = None
    for k, (r, f) in enumerate(units):
        u = u_next
        if k + 1 < len(units):
            u_next = up(*units[k + 1])
        part = down(u, f)
        acc = part if f == 0 else acc + part
        if f == n_ff - 1:
            post_mlp(r, acc)


def _out_mlp(x, ys, mod, w_out, w_up, w_down, g1, b1, g2, b2, tm, shared_mod):
    b, n, d = x.shape
    mod_map = (lambda i, t: (0, 0, 0)) if shared_mod else (lambda i, t: (i, 0, 0))
    full = lambda a: pl.BlockSpec(a.shape, lambda i, t: (0,) * a.ndim, pipeline_mode=pl.Buffered(1))
    vecs = [v.reshape(1, d) for v in (g1, b1, g2, b2)]
    return pl.pallas_call(
        _out_mlp_kernel,
        grid=(b, n // tm),
        in_specs=[pl.BlockSpec((1, tm, d), lambda i, t: (i, t, 0))]
        + [pl.BlockSpec((1, tm, GROUP_W), lambda i, t: (i, t, 0))] * 4
        + [pl.BlockSpec((1, 6, d), mod_map), full(w_out), full(w_up), full(w_down)]
        + [full(v) for v in vecs],
        out_specs=pl.BlockSpec((1, tm, d), lambda i, t: (i, t, 0)),
        out_shape=jax.ShapeDtypeStruct((b, n, d), F32),
        scratch_shapes=[pltpu.VMEM((tm, d), BF16)],
        compiler_params=pltpu.CompilerParams(dimension_semantics=("arbitrary", "arbitrary"),
                                             vmem_limit_bytes=VMEM_LIMIT),
        name="out_proj_mlp",
    )(x, *ys, mod, w_out, w_up, w_down, *vecs)


def _rope_tables(n):
    rows = n // GRID_W
    row = jnp.repeat(jnp.arange(rows), GRID_W).astype(F32)
    col = (jnp.arange(rows * GRID_W) % GRID_W).astype(F32)
    inv = ROPE_THETA ** (-jnp.arange(ROPE_FREQS, dtype=F32) / ROPE_FREQS)
    cos, sin = [], []
    for pos in (row, col):
        ang = pos[:, None] * inv
        cos += [jnp.cos(ang), jnp.cos(ang)]
        sin += [-jnp.sin(ang), jnp.sin(ang)]
    return tuple(jnp.tile(jnp.concatenate(t, axis=1), (1, LANES // HEAD_DIM)) for t in (cos, sin))


def kernel(x, c, ctx, c_ctx, w_ada, b_ada, w_in, ret_decay_logit, gqa_q_gain, gqa_k_gain,
           swa_sink, hgrn_lb, w_out, ln1_g, ln1_b, w_up, w_down, ln2_g, ln2_b):
    batch, n, d = x.shape
    depth = w_ada.shape[0]
    rope_tabs = _rope_tables(n)
    p_lb = jax.nn.softmax(hgrn_lb.astype(F32), axis=0)
    lower_bounds = jnp.cumsum(p_lb, axis=0) - p_lb[0]
    log_decay = jax.nn.log_sigmoid(ret_decay_logit.astype(F32))

    pad = (-(batch + 1)) % 8
    cvec = jnp.concatenate([c, c_ctx[None, :], jnp.zeros((pad, d), F32)], axis=0)
    mods = _modulation(cvec, w_ada, b_ada)
    w_in_bf, w_out_bf, w_up_bf, w_down_bf = (w.astype(BF16) for w in (w_in, w_out, w_up, w_down))

    xc = ctx
    for l in range(depth):
        ctx_out = l < depth - 1
        m_lat = mods[l, :batch].reshape(batch, 6, d)
        m_ctx = mods[l, batch:batch + 1].reshape(1, 6, d)
        p = _project(x, m_lat, w_in_bf[l], 1024, False)
        pc = _project(xc, m_ctx, w_in_bf[l], xc.shape[1], True)
        y_ret, yc_ret = _retention(p[0], pc[0], log_decay[l], rope_tabs, ctx_out)
        y_glb, yc_glb = _global_gqa(p[1], pc[1], gqa_q_gain[l], gqa_k_gain[l], rope_tabs, ctx_out)
        y_win, yc_win = _window_gqa(p[2], pc[2], swa_sink[l], rope_tabs, ctx_out)
        y_hg, yc_hg = _hgrn2(p[3], pc[3], lower_bounds[l], ctx_out)
        post = (w_out_bf[l], w_up_bf[l], w_down_bf[l], ln1_g[l], ln1_b[l], ln2_g[l], ln2_b[l])
        x = _out_mlp(x, (y_ret, y_glb, y_win, y_hg), m_lat, *post, 1024, False)
        if ctx_out:
            xc = _out_mlp(xc, (yc_ret, yc_glb, yc_win, yc_hg), m_ctx, *post, 256, True)
    return x
```

```python
import functools

import numpy as np
import jax
import jax.numpy as jnp
from jax import lax
from jax.experimental import pallas as pl
from jax.experimental.pallas import tpu as pltpu

F32 = jnp.float32
BF16 = jnp.bfloat16

D_MODEL = 1024
CTX_LEN = 256
GRID_W = 64
HEAD_DIM = 64
GROUP_W = 256
D_FF = 4 * D_MODEL
DEPTH = 2
ROPE_THETA = 10000.0
ROPE_FREQS = HEAD_DIM // 4
WINDOW = 128
ALPHA = (2 * DEPTH) ** 0.25
EPS = 1e-6
NEG = -1e30
TINY = 1e-30
QK_SCALE = HEAD_DIM ** -0.5

LANES = 128
CHUNK = 128
N_LEVELS = 7
VMEM_LIMIT = 56 * 1024 * 1024


def _dot(a, b):
    return jnp.dot(a, b, preferred_element_type=F32)


def _dot_nt(a, b):
    return lax.dot_general(a, b, (((1,), (1,)), ((), ())), preferred_element_type=F32)


def _dot_tn(a, b):
    return lax.dot_general(a, b, (((0,), (0,)), ((), ())), preferred_element_type=F32)


def _split_dot(x, w):
    hi = x.astype(BF16)
    lo = (x - hi.astype(F32)).astype(BF16)
    return _dot(hi, w) + _dot(lo, w)


def _silu(x):
    return x * jax.nn.sigmoid(x)


def _ln_rows(x):
    mu = jnp.mean(x, axis=-1, keepdims=True)
    xc = x - mu
    var = jnp.mean(xc * xc, axis=-1, keepdims=True)
    return xc * lax.rsqrt(var + EPS)


def _rope_roll(x, c, sn):
    first = (_iota(x.shape, 1) & ROPE_FREQS) == 0
    partner = jnp.where(first, pltpu.roll(x, LANES - ROPE_FREQS, 1), pltpu.roll(x, ROPE_FREQS, 1))
    return x * c + partner * sn


def _iota(shape, dim):
    return lax.broadcasted_iota(jnp.int32, shape, dim)


def _head_block_mask():
    return (_iota((LANES, LANES), 0) // HEAD_DIM) == (_iota((LANES, LANES), 1) // HEAD_DIM)


def _stack_heads(qb):
    lo = _iota(qb.shape, 1) < HEAD_DIM
    zero = jnp.zeros_like(qb)
    return jnp.concatenate([jnp.where(lo, qb, zero), jnp.where(lo, zero, qb)], axis=0)


def _unstack_heads(r):
    t = r.shape[0] // 2
    lo = _iota((t, LANES), 1) < HEAD_DIM
    return jnp.where(lo, r[:t], r[t:])


def _mod_kernel(c_ref, w_ref, b_ref, o_ref):
    s = _silu(c_ref[...])
    o_ref[0] = _dot(s.astype(BF16), w_ref[0].astype(BF16)) + b_ref[0]


def _modulation(cvec, w_ada, b_ada):
    rows = cvec.shape[0]
    depth, d, n = w_ada.shape
    nt = n // d
    return pl.pallas_call(
        _mod_kernel,
        grid=(depth, nt),
        in_specs=[pl.BlockSpec((rows, d), lambda l, j: (0, 0)),
                  pl.BlockSpec((1, d, d), lambda l, j: (l, 0, j)),
                  pl.BlockSpec((1, 1, d), lambda l, j: (l, 0, j))],
        out_specs=pl.BlockSpec((1, rows, d), lambda l, j: (l, 0, j)),
        out_shape=jax.ShapeDtypeStruct((depth, rows, n), F32),
        compiler_params=pltpu.CompilerParams(dimension_semantics=("arbitrary", "arbitrary"),
                                             vmem_limit_bytes=VMEM_LIMIT),
        name="modulation",
    )(cvec, w_ada, b_ada.reshape(depth, 1, n))


PROJ_GROUPS = (4 * GROUP_W, 2 * GROUP_W, 2 * GROUP_W, 5 * GROUP_W)
PROJ_COLS = 256


def _proj_kernel(x_ref, mod_ref, w_ref, o_ret, o_glb, o_win, o_hg, h_scr):
    half = x_ref.shape[1] // 2
    halves = (slice(0, half), slice(half, 2 * half))

    def modulate(r):
        h_scr[r, :] = (_ln_rows(x_ref[0, r, :]) * (1.0 + mod_ref[0, 1:2, :]) + mod_ref[0, 0:1, :]).astype(BF16)

    def project(r, first_chunk_only):
        col = 0
        for o_ref, width in zip((o_ret, o_glb, o_win, o_hg), PROJ_GROUPS):
            for j in range(width // PROJ_COLS):
                if first_chunk_only == (col == 0):
                    o_ref[0, r, j * PROJ_COLS:(j + 1) * PROJ_COLS] = _dot(
                        h_scr[r, :], w_ref[:, col:col + PROJ_COLS]).astype(BF16)
                col += PROJ_COLS

    modulate(halves[0])
    project(halves[0], True)
    modulate(halves[1])
    project(halves[0], False)
    project(halves[1], True)
    project(halves[1], False)


def _project(x, mod, w_in_bf, tm, shared_mod):
    b, n, d = x.shape
    n_in = w_in_bf.shape[1]
    mod_map = (lambda i, t: (0, 0, 0)) if shared_mod else (lambda i, t: (i, 0, 0))
    return pl.pallas_call(
        _proj_kernel,
        grid=(b, n // tm),
        in_specs=[pl.BlockSpec((1, tm, d), lambda i, t: (i, t, 0)),
                  pl.BlockSpec((1, 6, d), mod_map),
                  pl.BlockSpec((d, n_in), lambda i, t: (0, 0), pipeline_mode=pl.Buffered(1))],
        out_specs=[pl.BlockSpec((1, tm, w), lambda i, t: (i, t, 0)) for w in PROJ_GROUPS],
        out_shape=[jax.ShapeDtypeStruct((b, n, w), BF16) for w in PROJ_GROUPS],
        scratch_shapes=[pltpu.VMEM((tm, d), BF16)],
        compiler_params=pltpu.CompilerParams(dimension_semantics=("arbitrary", "arbitrary"),
                                             vmem_limit_bytes=VMEM_LIMIT),
        name="adaln_in_proj",
    )(x, mod, w_in_bf)


RET_GROUP = 8


def _ret_kernel(lg_ref, q_ref, k_ref, v_ref, g_ref, qc_ref, kc_ref, vc_ref, gc_ref,
                cos_ref, sin_ref, *rest, ctx_out):
    if ctx_out:
        y_ref, yc_ref, o_scr, qs_scr, ks_scr = rest
    else:
        y_ref, o_scr, qs_scr, ks_scr = rest
        yc_ref = None
    j = pl.program_id(1)
    c_len = CHUNK
    n_lat = q_ref.shape[1] // c_len
    n_ctx = qc_ref.shape[1] // c_len

    hi1 = _iota((1, LANES), 1) >= HEAD_DIM
    rowi = _iota((c_len, LANES), 0)
    coli = _iota((c_len, LANES), 1)
    pos = rowi.astype(F32)
    lgf = jnp.where(hi1, lg_ref[0, 2 * j + 1], lg_ref[0, 2 * j])
    lgb = jnp.where(hi1, lg_ref[1, 2 * j + 1], lg_ref[1, 2 * j])
    wqf = jnp.exp(lgf * (pos + 1.0))
    wqb = jnp.exp(lgb * (c_len - pos))
    wkf = jnp.exp(lgf * (c_len - 1.0 - pos))
    wkb = jnp.exp(lgb * pos)
    dcf = jnp.exp(lgf * c_len)
    dcb = jnp.exp(lgb * c_len)
    dlt = (rowi - coli).astype(F32)

    def head_mask(e):
        lf = lg_ref[0, 2 * j + e]
        lb = lg_ref[1, 2 * j + e]
        return (jnp.where(dlt >= 0, jnp.exp(lf * jnp.maximum(dlt, 0.0)), 0.0)
                + jnp.where(dlt <= 0, jnp.exp(lb * jnp.maximum(-dlt, 0.0)), 0.0))

    m2 = jnp.concatenate([head_mask(0), head_mask(1)], axis=0)
    bd = _head_block_mask()
    bdm = jnp.where(bd, 1.0 / HEAD_DIM, 0.0).astype(BF16)


    def state_chain(st, us, decay):
        seen = []
        for u in us:
            seen.append(st)
            st = st * decay + jnp.where(bd, u, 0.0)
        return seen, st

    def fwd_group(cs, qkv, st, need_out):
        k8s = [k * QK_SCALE for _, k, _ in qkv]
        us = [_dot_tn(v, (k8 * wkf).astype(BF16)) for (_, _, v), k8 in zip(qkv, k8s)]
        seen, st = state_chain(st, us, dcf)
        if need_out:
            a2s = [_dot_nt(_stack_heads(q.astype(BF16)), k8.astype(BF16)) for (q, _, _), k8 in zip(qkv, k8s)]
            rs = [_dot((a2 * m2).astype(BF16), v) for a2, (_, _, v) in zip(a2s, qkv)]
            inter = [_dot_nt((q * wqf).astype(BF16), s.astype(BF16)) for (q, _, _), s in zip(qkv, seen)]
            for c, r, x in zip(cs, rs, inter):
                o_scr[pl.ds(c * c_len, c_len), :] = _unstack_heads(r) + x
        for c, (q, _, _), k8 in zip(cs, qkv, k8s):
            qs_scr[pl.ds(c * c_len, c_len), :] = q
            ks_scr[pl.ds(c * c_len, c_len), :] = k8
        return st

    def bwd_group(cs, vgs, st, out_ref, row0s, need_out):
        qs = [qs_scr[pl.ds(c * c_len, c_len), :] for c in cs]
        k8s = [ks_scr[pl.ds(c * c_len, c_len), :] for c in cs]
        us = [_dot_tn(v, (k8 * wkb).astype(BF16)) for (v, _), k8 in zip(vgs, k8s)]
        seen, st = state_chain(st, us, dcb)
        if need_out:
            os_ = [o_scr[pl.ds(c * c_len, c_len), :] + _dot_nt((q * wqb).astype(BF16), s.astype(BF16))
                   for c, q, s in zip(cs, qs, seen)]
            xcs = [o - _split_dot(o, bdm) for o in os_]
            vars_ = [_dot((xc * xc).astype(BF16), bdm) for xc in xcs]
            for row0, xc, var, (_, g) in zip(row0s, xcs, vars_, vgs):
                out_ref[0, pl.ds(row0, c_len), :] = (xc * lax.rsqrt(var + EPS) * _silu(g)).astype(out_ref.dtype)
        return st

    ctx_rows = [slice(c * c_len, (c + 1) * c_len) for c in range(n_ctx)]
    st = fwd_group(list(range(n_ctx)),
                   [(qc_ref[0, r, :].astype(F32), kc_ref[0, r, :].astype(F32), vc_ref[0, r, :]) for r in ctx_rows],
                   jnp.zeros((LANES, LANES), F32), ctx_out)

    def lat_fwd(i, st):
        idx = [i * RET_GROUP + k for k in range(RET_GROUP)]
        qkv = []
        for ci in idx:
            rows = pl.ds(pl.multiple_of(ci * c_len, c_len), c_len)
            cs, sn = cos_ref[rows, :], sin_ref[rows, :]
            qkv.append((_rope_roll(q_ref[0, rows, :].astype(F32), cs, sn),
                        _rope_roll(k_ref[0, rows, :].astype(F32), cs, sn), v_ref[0, rows, :]))
        return fwd_group([ci + n_ctx for ci in idx], qkv, st, True)

    lax.fori_loop(0, n_lat // RET_GROUP, lat_fwd, st)

    st = bwd_group(list(reversed(range(n_ctx))),
                   [(vc_ref[0, r, :], gc_ref[0, r, :].astype(F32) if ctx_out else None) for r in reversed(ctx_rows)],
                   jnp.zeros((LANES, LANES), F32), yc_ref, [c * c_len for c in reversed(range(n_ctx))], ctx_out)

    def lat_bwd(i, st):
        idx = [n_lat - 1 - i * RET_GROUP - k for k in range(RET_GROUP)]
        row0s = [pl.multiple_of(ci * c_len, c_len) for ci in idx]
        vgs = [(v_ref[0, pl.ds(r0, c_len), :], g_ref[0, pl.ds(r0, c_len), :].astype(F32)) for r0 in row0s]
        return bwd_group([ci + n_ctx for ci in idx], vgs, st, y_ref, row0s, True)

    lax.fori_loop(0, n_lat // RET_GROUP, lat_bwd, st)


def _lane_block(n, idx):
    return pl.BlockSpec((1, n, LANES), lambda i, j, idx=idx: (i, 0, idx + j))


def _retention(p, pc, log_decay, rope_tabs, ctx_out):
    b, n, _ = p.shape
    nc = pc.shape[1]
    tab = pl.BlockSpec((n, LANES), lambda i, j: (0, 0))
    out_specs = [pl.BlockSpec((1, n, LANES), lambda i, j: (i, 0, j))]
    out_shape = [jax.ShapeDtypeStruct((b, n, GROUP_W), BF16)]
    if ctx_out:
        out_specs.append(pl.BlockSpec((1, nc, LANES), lambda i, j: (i, 0, j)))
        out_shape.append(jax.ShapeDtypeStruct((b, nc, GROUP_W), BF16))
    res = pl.pallas_call(
        functools.partial(_ret_kernel, ctx_out=ctx_out),
        grid=(b, 2),
        in_specs=[pl.BlockSpec(memory_space=pltpu.SMEM)]
        + [_lane_block(n, 2 * s) for s in range(4)] + [_lane_block(nc, 2 * s) for s in range(4)]
        + [tab, tab],
        out_specs=out_specs,
        out_shape=out_shape,
        scratch_shapes=[pltpu.VMEM((n + nc, LANES), F32)] * 3,
        compiler_params=pltpu.CompilerParams(dimension_semantics=("arbitrary", "arbitrary"),
                                             vmem_limit_bytes=VMEM_LIMIT),
        name="retention",
    )(log_decay, p, p, p, p, pc, pc, pc, pc, *rope_tabs)
    return res if ctx_out else (res[0], None)


def _own_head_dup(x, j):
    own = (_iota(x.shape, 1) // HEAD_DIM) == j
    return jnp.where(own, x, pltpu.roll(x, HEAD_DIM, 1))


def _values_with_ones(v, j):
    lo = _iota(v.shape, 1) < HEAD_DIM
    return jnp.where(lo, _own_head_dup(v, j), 1.0)


def _normalize_heads(r):
    t = r.shape[0] // 2
    top, bot = r[:t], r[t:]
    lo = _iota((t, LANES), 1) < HEAD_DIM
    return jnp.where(lo, top / pltpu.roll(top, HEAD_DIM, 1), pltpu.roll(bot, HEAD_DIM, 1) / bot)


GLB_KEYS = 768
GLB_TILE = 256
GLB_GROUP = 8
GLB_BOUND_SLACK = 1.01
GLB_MAX_SHIFT = 40.0


def _rms_heads(x, bdm, gain):
    return x * lax.rsqrt(_split_dot(x * x, bdm) + EPS) * gain


def _glb_kernel(shift_ref, q_ref, k_ref, v_ref, qc_ref, kc_ref, vc_ref, qg_ref, kg_ref,
                qcos_ref, qsin_ref, kcos_ref, ksin_ref, *rest, ctx_out, bounded):
    n = q_ref.shape[1]
    nc = qc_ref.shape[1]
    if ctx_out:
        y_ref, yc_ref, kd_scr, vd_scr = rest[:4]
    else:
        y_ref, kd_scr, vd_scr = rest[:3]
        yc_ref = None
    if bounded:
        qa_scr, = rest[-1:]
    else:
        q2_scr, s_scr = rest[-2:]
    j = pl.program_id(1)
    tq = CHUNK
    bdm = jnp.where(_head_block_mask(), 1.0 / HEAD_DIM, 0.0).astype(BF16)
    qg = qg_ref[...]
    kg = kg_ref[...]
    def normed_rope(xb, cos_tab, sin_tab):
        x = xb.astype(F32)
        return lax.rsqrt(_split_dot(x * x, bdm) + EPS) * _rope_roll(x, cos_tab, sin_tab)

    for t in range(nc // tq):
        rows = slice(t * tq, (t + 1) * tq)
        kn = _rms_heads(kc_ref[0, rows, :].astype(F32), bdm, kg)
        kd_scr[rows, :] = _own_head_dup(kn, j).astype(BF16)
        vd_scr[rows, :] = _values_with_ones(vc_ref[0, rows, :].astype(F32), j).astype(BF16)

    def prep(t, carry):
        rows = pl.ds(pl.multiple_of(t * tq, tq), tq)
        dst = pl.ds(pl.multiple_of(nc + t * tq, tq), tq)
        x = k_ref[0, rows, :].astype(F32)
        kn = lax.rsqrt(_split_dot(x * x, bdm) + EPS) * _rope_roll(x, kcos_ref[rows, :], ksin_ref[rows, :])
        kd_scr[dst, :] = _own_head_dup(kn, j).astype(BF16)
        vd_scr[dst, :] = _values_with_ones(v_ref[0, rows, :].astype(F32), j).astype(BF16)
        return carry

    lax.fori_loop(0, n // tq, prep, 0, unroll=4)

    n_tiles = n // tq

    if bounded:
        shift = shift_ref[0]

        def stack_queries(t, carry):
            rows = pl.ds(pl.multiple_of(t * tq, tq), tq)
            qn = normed_rope(q_ref[0, rows, :], qcos_ref[rows, :], qsin_ref[rows, :])
            qa_scr[t] = _stack_heads((qn * QK_SCALE).astype(BF16))
            return carry

        lax.fori_loop(0, n_tiles, stack_queries, 0, unroll=4)

        def bounded_group(i, carry):
            tiles = [i * GLB_GROUP + k for k in range(GLB_GROUP)]
            q2s = [qa_scr[t] for t in tiles]
            steps = [(g, k0) for g in range(GLB_GROUP) for k0 in range(0, n + nc, GLB_TILE)]

            def scores(g, k0):
                return _dot_nt(q2s[g], kd_scr[k0:k0 + GLB_TILE, :])

            s = scores(*steps[0])
            acc = None
            for idx, (g, k0) in enumerate(steps):
                if idx + 1 < len(steps):
                    s_next = scores(*steps[idx + 1])
                pv = _dot(jnp.exp(s - shift).astype(BF16), vd_scr[k0:k0 + GLB_TILE, :])
                acc = pv if k0 == 0 else acc + pv
                s = s_next
                if k0 + GLB_TILE >= n + nc:
                    rows = pl.ds(pl.multiple_of(tiles[g] * tq, tq), tq)
                    y_ref[0, rows, :] = _normalize_heads(acc).astype(y_ref.dtype)
            return carry

        lax.fori_loop(0, n_tiles // GLB_GROUP, bounded_group, 0)

    n_chunks = (n + nc) // GLB_KEYS

    def scores(q2, c):
        return _dot_nt(q2, kd_scr[c * GLB_KEYS:(c + 1) * GLB_KEYS, :])

    def stacked_q(t):
        rows = pl.ds(pl.multiple_of(t * tq, tq), tq)
        qn = normed_rope(q_ref[0, rows, :], qcos_ref[rows, :], qsin_ref[rows, :])
        return _stack_heads((qn * QK_SCALE).astype(BF16))

    def online_step(s, c, m, r):
        m_c = jnp.max(s, axis=-1, keepdims=True)
        m_new = m_c if m is None else jnp.maximum(m, m_c)
        pv = _dot(jnp.exp(s - m_new).astype(BF16), vd_scr[c * GLB_KEYS:(c + 1) * GLB_KEYS, :])
        return m_new, (pv if m is None else r * jnp.exp(m - m_new) + pv)

    def q_tile(t, carry):
        q2 = q2_scr[...]
        s = s_scr[...]
        m = r = None
        for c in range(n_chunks):
            if c + 1 < n_chunks:
                s_next = scores(q2, c + 1)
            else:
                q2_next = stacked_q(jnp.minimum(t + 1, n_tiles - 1))
                q2_scr[...] = q2_next
                s_next = scores(q2_next, 0)
            m, r = online_step(s, c, m, r)
            s = s_next
        s_scr[...] = s
        y_ref[0, pl.ds(pl.multiple_of(t * tq, tq), tq), :] = _normalize_heads(r).astype(y_ref.dtype)
        return carry

    if not bounded:
        q2_scr[...] = stacked_q(0)
        s_scr[...] = scores(q2_scr[...], 0)
        lax.fori_loop(0, n_tiles, q_tile, 0)

    if ctx_out:
        for t in range(nc // tq):
            rows = slice(t * tq, (t + 1) * tq)
            qn = _rms_heads(qc_ref[0, rows, :].astype(F32), bdm, qg)
            s = _dot_nt(_stack_heads((qn * QK_SCALE).astype(BF16)), kd_scr[0:nc, :])
            e = jnp.exp(s - jnp.max(s, axis=-1, keepdims=True))
            yc_ref[0, rows, :] = _normalize_heads(_dot(e.astype(BF16), vd_scr[0:nc, :])).astype(yc_ref.dtype)


def _global_gqa(p, pc, q_gain, k_gain, rope_tabs, ctx_out):
    b, n, _ = p.shape
    nc = pc.shape[1]
    tab = pl.BlockSpec((n, LANES), lambda i, j: (0, 0))
    gain = pl.BlockSpec((1, LANES), lambda i, j: (0, 0))
    kv = lambda rows, idx: pl.BlockSpec((1, rows, LANES), lambda i, j, idx=idx: (i, 0, idx))
    out_specs = [pl.BlockSpec((1, n, LANES), lambda i, j: (i, 0, j))]
    out_shape = [jax.ShapeDtypeStruct((b, n, GROUP_W), BF16)]
    if ctx_out:
        out_specs.append(pl.BlockSpec((1, nc, LANES), lambda i, j: (i, 0, j)))
        out_shape.append(jax.ShapeDtypeStruct((b, nc, GROUP_W), BF16))
    def call(bounded):
        return pl.pallas_call(
            functools.partial(_glb_kernel, ctx_out=ctx_out, bounded=bounded),
            grid=(b, 2),
            in_specs=[pl.BlockSpec(memory_space=pltpu.SMEM),
                      _lane_block(n, 0), kv(n, 2), kv(n, 3), _lane_block(nc, 0), kv(nc, 2), kv(nc, 3),
                      gain, gain, tab, tab, tab, tab],
            out_specs=out_specs,
            out_shape=out_shape,
            scratch_shapes=[pltpu.VMEM((n + nc, LANES), BF16)] * 2
            + ([pltpu.VMEM((n // CHUNK, 2 * CHUNK, LANES), BF16)] if bounded else
               [pltpu.VMEM((2 * CHUNK, LANES), BF16), pltpu.VMEM((2 * CHUNK, GLB_KEYS), F32)]),
            compiler_params=pltpu.CompilerParams(dimension_semantics=("arbitrary", "arbitrary"),
                                                 vmem_limit_bytes=VMEM_LIMIT),
            name="global_gqa" if bounded else "global_gqa_exact_max",
        )

    shift = (HEAD_DIM * QK_SCALE * GLB_BOUND_SLACK) * jnp.max(jnp.abs(q_gain)) * jnp.max(jnp.abs(k_gain))
    cos_tab, sin_tab = rope_tabs
    qg2, kg2 = jnp.tile(q_gain, 2)[None, :], jnp.tile(k_gain, 2)[None, :]
    partner = lambda g: g.reshape(1, -1, 2, ROPE_FREQS)[:, :, ::-1, :].reshape(1, LANES)
    args = (shift.reshape(1).astype(F32), p, p, p, pc, pc, pc, qg2, kg2,
            cos_tab * qg2, sin_tab * partner(qg2), cos_tab * kg2, sin_tab * partner(kg2))
    res = lax.cond(shift <= GLB_MAX_SHIFT, lambda a: call(True)(*a), lambda a: call(False)(*a), args)
    return res if ctx_out else (res[0], None)


WIN_GROUP = 2


def _win_kernel(sink_ref, q_ref, k_ref, v_ref, qc_ref, kc_ref, vc_ref,
                cos_ref, sin_ref, *rest, ctx_out):
    if ctx_out:
        y_ref, yc_ref, kd_scr, vd_scr, qa_scr = rest
    else:
        y_ref, kd_scr, vd_scr, qa_scr = rest
        yc_ref = None
    j = pl.program_id(1)
    n = q_ref.shape[1]
    nc = qc_ref.shape[1]
    tq = CHUNK
    nb = n // tq
    lat0 = nc + tq

    ones = jnp.ones((LANES, LANES), BF16)

    def sq_norms(xb, acc):
        xf = xb.astype(F32)
        return jnp.maximum(acc, _dot((xf * xf).astype(BF16), ones))

    zeros = jnp.zeros((tq, LANES), BF16)
    for r0 in (nc, lat0 + n):
        kd_scr[r0:r0 + tq, :] = zeros
        vd_scr[r0:r0 + tq, :] = zeros
    ksq_max = jnp.zeros((tq, LANES), F32)
    for t in range(nc // tq):
        rows = slice(t * tq, (t + 1) * tq)
        kd = _own_head_dup(kc_ref[0, rows, :].astype(F32), j).astype(BF16)
        kd_scr[rows, :] = kd
        ksq_max = sq_norms(kd, ksq_max)
        vd_scr[rows, :] = _values_with_ones(vc_ref[0, rows, :].astype(F32), j).astype(BF16)

    def prep(t, ksq_max):
        rows = pl.ds(pl.multiple_of(t * tq, tq), tq)
        dst = pl.ds(pl.multiple_of(lat0 + t * tq, tq), tq)
        kn = _rope_roll(k_ref[0, rows, :].astype(F32), cos_ref[rows, :], sin_ref[rows, :])
        kd = _own_head_dup(kn, j).astype(BF16)
        kd_scr[dst, :] = kd
        vd_scr[dst, :] = _values_with_ones(v_ref[0, rows, :].astype(F32), j).astype(BF16)
        return sq_norms(kd, ksq_max)

    ksq_max = lax.fori_loop(0, nb, prep, ksq_max, unroll=4)

    sk = jnp.where(_iota((2 * tq, 1), 0) < tq, sink_ref[2 * j], sink_ref[2 * j + 1])

    def stack_queries(t, qsq_max):
        rows = pl.ds(pl.multiple_of(t * tq, tq), tq)
        qn = _rope_roll(q_ref[0, rows, :].astype(F32), cos_ref[rows, :], sin_ref[rows, :])
        q2 = _stack_heads((qn * QK_SCALE).astype(BF16))
        qa_scr[t] = q2
        return sq_norms(q2, qsq_max)

    qsq_max = lax.fori_loop(0, nb, stack_queries, jnp.zeros((2 * tq, LANES), F32), unroll=4)
    bound_max = jnp.sqrt(jnp.max(qsq_max, keepdims=True) * (0.5 * jnp.max(ksq_max, keepdims=True))) * GLB_BOUND_SLACK
    shift_all = jnp.maximum(bound_max, sk)

    sum_lanes = _iota((2 * tq, LANES), 1) >= HEAD_DIM

    def softmax_pv(parts, sk, shift):
        if shift is None:
            shift = sk
            for s, _ in parts:
                shift = jnp.maximum(shift, jnp.max(s, axis=-1, keepdims=True))
        acc = jnp.where(sum_lanes, jnp.exp(sk - shift), 0.0)
        for s, vals in parts:
            acc = acc + _dot(jnp.exp(s - shift).astype(BF16), vals)
        return _normalize_heads(acc)

    band_c = _iota((2 * tq, 3 * tq), 1)
    off = band_c - _iota((2 * tq, 3 * tq), 0) % tq
    band_bias = jnp.where((off >= 0) & (off <= 2 * WINDOW), 0.0, NEG)
    bias = {"inner": band_bias,
            "first": jnp.where(band_c < tq, NEG, band_bias),
            "last": jnp.where(band_c >= 2 * tq, NEG, band_bias)}

    def tile_group(tiles, bounded):
        def aligned(start):
            return start if isinstance(start, int) else pl.multiple_of(start, tq)

        q2s = [qa_scr[t] for t, _ in tiles]
        bands = [pl.ds(aligned(nc + t * tq), 3 * tq) for t, _ in tiles]
        s_ctx = [_dot_nt(q2, kd_scr[0:nc, :]) for q2 in q2s]
        s_band = [_dot_nt(q2, kd_scr[band, :]) + bias[kind] for q2, band, (_, kind) in zip(q2s, bands, tiles)]
        for (t, _), band, sc, sb in zip(tiles, bands, s_ctx, s_band):
            out = softmax_pv([(sc, vd_scr[0:nc, :]), (sb, vd_scr[band, :])], sk, shift_all if bounded else None)
            y_ref[0, pl.ds(aligned(t * tq), tq), :] = out.astype(y_ref.dtype)

    def all_tiles(bounded):
        tile_group([(0, "first"), (nb - 1, "last")], bounded)

        def inner(i, carry):
            tile_group([(1 + WIN_GROUP * i + k, "inner") for k in range(WIN_GROUP)], bounded)
            return carry

        lax.fori_loop(0, (nb - 2) // WIN_GROUP, inner, 0)

    small_logits = jnp.max(bound_max) <= GLB_MAX_SHIFT

    @pl.when(small_logits)
    def _():
        all_tiles(True)

    @pl.when(jnp.logical_not(small_logits))
    def _():
        all_tiles(False)

    if ctx_out:
        for t in range(nc // tq):
            rows = slice(t * tq, (t + 1) * tq)
            q2 = _stack_heads((qc_ref[0, rows, :].astype(F32) * QK_SCALE).astype(BF16))
            s_ctx = _dot_nt(q2, kd_scr[0:nc, :])
            yc_ref[0, rows, :] = softmax_pv([(s_ctx, vd_scr[0:nc, :])], sk, None).astype(yc_ref.dtype)


def _window_gqa(p, pc, sink, rope_tabs, ctx_out):
    b, n, _ = p.shape
    nc = pc.shape[1]
    tab = pl.BlockSpec((n, LANES), lambda i, j: (0, 0))
    kv = lambda rows, idx: pl.BlockSpec((1, rows, LANES), lambda i, j, idx=idx: (i, 0, idx))
    out_specs = [pl.BlockSpec((1, n, LANES), lambda i, j: (i, 0, j))]
    out_shape = [jax.ShapeDtypeStruct((b, n, GROUP_W), BF16)]
    if ctx_out:
        out_specs.append(pl.BlockSpec((1, nc, LANES), lambda i, j: (i, 0, j)))
        out_shape.append(jax.ShapeDtypeStruct((b, nc, GROUP_W), BF16))
    res = pl.pallas_call(
        functools.partial(_win_kernel, ctx_out=ctx_out),
        grid=(b, 2),
        in_specs=[pl.BlockSpec(memory_space=pltpu.SMEM),
                  _lane_block(n, 0), kv(n, 2), kv(n, 3), _lane_block(nc, 0), kv(nc, 2), kv(nc, 3),
                  tab, tab],
        out_specs=out_specs,
        out_shape=out_shape,
        scratch_shapes=[pltpu.VMEM((n + nc + 2 * CHUNK, LANES), BF16)] * 2
        + [pltpu.VMEM((n // CHUNK, 2 * CHUNK, LANES), BF16)],
        compiler_params=pltpu.CompilerParams(dimension_semantics=("arbitrary", "arbitrary"),
                                             vmem_limit_bytes=VMEM_LIMIT),
        name="window_gqa",
    )(sink, p, p, p, pc, pc, pc, *rope_tabs)
    return res if ctx_out else (res[0], None)


def _mid_row_index(level, rev):
    h = 1 << level
    t = np.arange(CHUNK)
    return (t // (2 * h)) * (2 * h) + (h if rev else h - 1)


def _mid_rows(b, level, rev):
    h = 1 << level
    idx = _mid_row_index(level, rev)
    pieces = [jnp.broadcast_to(b[int(idx[m]):int(idx[m]) + 1, :], (2 * h, b.shape[1])) for m in range(0, CHUNK, 2 * h)]
    return pieces[0] if len(pieces) == 1 else jnp.concatenate(pieces, axis=0)


HG_MATMUL_LEVELS = 3


def _cumsum_matrices():
    u = np.arange(CHUNK)[None, :]
    out = np.zeros((2, (1 + HG_MATMUL_LEVELS) * CHUNK, CHUNK), np.float32)
    for d, rev in enumerate((False, True)):
        rows = [np.arange(CHUNK)] + [_mid_row_index(level, rev) for level in range(HG_MATMUL_LEVELS)]
        for k, r in enumerate(rows):
            out[d, k * CHUNK:(k + 1) * CHUNK] = (u >= r[:, None]) if rev else (u <= r[:, None])
    return out


HG_GROUP = 8


def _level_masks():
    t = np.arange(2 * CHUNK)[:, None] % CHUNK
    s = np.arange(CHUNK)[None, :]
    out = np.zeros((2, N_LEVELS, 2 * CHUNK, CHUNK), np.float32)
    for level in range(N_LEVELS):
        same = (t >> (level + 1)) == (s >> (level + 1))
        t_hi, s_hi = (t >> level) & 1, (s >> level) & 1
        out[0, level] = same & (t_hi == 1) & (s_hi == 0)
        out[1, level] = same & (t_hi == 0) & (s_hi == 1)
    return out


def _hg_kernel(lvl_ref, cum_ref, lb_ref, q_ref, zf_ref, zb_ref, i_ref, g_ref,
               qc_ref, zfc_ref, zbc_ref, ic_ref, gc_ref, *rest, ctx_out):
    if ctx_out:
        y_ref, yc_ref, o_scr = rest
    else:
        y_ref, o_scr = rest
        yc_ref = None
    c_len = CHUNK
    n_lat = q_ref.shape[1] // c_len
    n_ctx = qc_ref.shape[1] // c_len
    lb = lb_ref[0]
    bd = _head_block_mask()
    bd_ones = jnp.where(bd, 1.0, 0.0).astype(BF16)
    bdm = jnp.where(bd, 1.0 / HEAD_DIM, 0.0).astype(BF16)


    def gates(z, rev):
        sg = jax.nn.sigmoid(z.astype(F32))
        f = lb + (1.0 - lb) * sg
        x = jnp.log(jnp.maximum(f, TINY))
        kk = (1.0 - lb) * (1.0 - sg)
        cum = _split_dot_lhs(cum_ref[1 if rev else 0], x)
        return kk, [cum[k * c_len:(k + 1) * c_len] for k in range(1 + HG_MATMUL_LEVELS)]

    def intra_scores(qraw, kk, cum, v, rev):
        b = cum[0]
        qs = _silu(qraw.astype(F32))
        o = _split_dot(qs * kk, bd_ones) * v.astype(F32)
        qs_b, kk_b = qs.astype(BF16), kk.astype(BF16)
        a = None
        for level in range(N_LEVELS):
            mid = cum[1 + level] if level < HG_MATMUL_LEVELS else _mid_rows(b, level, rev)
            e = jnp.exp(-jnp.abs(b - mid)).astype(BF16)
            al = _dot_nt(_stack_heads(qs_b * e), kk_b * e).astype(BF16) * lvl_ref[1 if rev else 0, level]
            a = al if a is None else a + al
        return qs, o, a

    def with_state(qs, o, kk, b, v, st, rev):
        edge = b[0:1, :] if rev else b[c_len - 1:c_len, :]
        if o is not None:
            o = o + _dot_nt((qs * jnp.exp(b)).astype(BF16), st.astype(BF16))
        u = _dot_tn(v, (kk * jnp.exp(edge - b)).astype(BF16))
        return o, st * jnp.exp(edge) + jnp.where(bd, u, 0.0)

    def run_group(chunks, st, rev, need_out):
        g = [gates(z, rev) for _, z, _ in chunks]
        if need_out:
            s = [intra_scores(q, kk, cum, v, rev) for (q, _, v), (kk, cum) in zip(chunks, g)]
            s = [(qs, o + _unstack_heads(_dot(a, v))) for (qs, o, a), (_, _, v) in zip(s, chunks)]
        else:
            s = [(None, None)] * len(chunks)
        outs = []
        for (qs, o), (kk, cum), (_, _, v) in zip(s, g, chunks):
            o, st = with_state(qs, o, kk, cum[0], v, st, rev)
            outs.append(o)
        return outs, st

    def finish(o, g):
        return o * lax.rsqrt(_dot((o * o).astype(BF16), bdm) + EPS) * _silu(g.astype(F32))

    for rev in (False, True):
        z_ref, zc_ref = (zb_ref, zbc_ref) if rev else (zf_ref, zfc_ref)
        order = list(reversed(range(n_ctx))) if rev else list(range(n_ctx))
        ctx_rows = [slice(c * c_len, (c + 1) * c_len) for c in order]
        outs, st = run_group([(qc_ref[0, r, :], zc_ref[0, r, :], ic_ref[0, r, :]) for r in ctx_rows],
                             jnp.zeros((LANES, LANES), F32), rev, ctx_out)
        if ctx_out:
            for r, o in zip(ctx_rows, outs):
                if rev:
                    yc_ref[0, r, :] = finish(o_scr[r, :] + o, gc_ref[0, r, :]).astype(yc_ref.dtype)
                else:
                    o_scr[r, :] = o

        def lat(i, st, rev=rev, z_ref=z_ref):
            first = (n_lat - 1 - i * HG_GROUP) if rev else i * HG_GROUP
            idx = [first - k if rev else first + k for k in range(HG_GROUP)]
            rows = [pl.ds(pl.multiple_of(ci * c_len, c_len), c_len) for ci in idx]
            srows = [pl.ds(pl.multiple_of((ci + n_ctx) * c_len, c_len), c_len) for ci in idx]
            outs, st = run_group([(q_ref[0, r, :], z_ref[0, r, :], i_ref[0, r, :]) for r in rows], st, rev, True)
            for r, sr, o in zip(rows, srows, outs):
                if rev:
                    y_ref[0, r, :] = finish(o_scr[sr, :] + o, g_ref[0, r, :]).astype(y_ref.dtype)
                else:
                    o_scr[sr, :] = o
            return st

        lax.fori_loop(0, n_lat // HG_GROUP, lat, st)


def _split_dot_lhs(w, x):
    hi = x.astype(BF16)
    lo = (x - hi.astype(F32)).astype(BF16)
    return _dot(w, hi) + _dot(w, lo)


def _hgrn2(p, pc, lower_bound, ctx_out):
    b, n, _ = p.shape
    nc = pc.shape[1]
    out_specs = [pl.BlockSpec((1, n, LANES), lambda i, j: (i, 0, j))]
    out_shape = [jax.ShapeDtypeStruct((b, n, GROUP_W), BF16)]
    if ctx_out:
        out_specs.append(pl.BlockSpec((1, nc, LANES), lambda i, j: (i, 0, j)))
        out_shape.append(jax.ShapeDtypeStruct((b, nc, GROUP_W), BF16))
    res = pl.pallas_call(
        functools.partial(_hg_kernel, ctx_out=ctx_out),
        grid=(b, 2),
        in_specs=[pl.BlockSpec((2, N_LEVELS, 2 * CHUNK, CHUNK), lambda i, j: (0, 0, 0, 0)),
                  pl.BlockSpec((2, (1 + HG_MATMUL_LEVELS) * CHUNK, CHUNK), lambda i, j: (0, 0, 0)),
                  pl.BlockSpec((1, 1, LANES), lambda i, j: (j, 0, 0))]
        + [_lane_block(n, 2 * s) for s in range(5)] + [_lane_block(nc, 2 * s) for s in range(5)],
        out_specs=out_specs,
        out_shape=out_shape,
        scratch_shapes=[pltpu.VMEM((n + nc, LANES), F32)],
        compiler_params=pltpu.CompilerParams(dimension_semantics=("arbitrary", "arbitrary"),
                                             vmem_limit_bytes=VMEM_LIMIT),
        name="hgrn2",
    )(jnp.asarray(_level_masks(), BF16), jnp.asarray(_cumsum_matrices(), BF16), lower_bound.reshape(2, 1, LANES),
      p, p, p, p, p, pc, pc, pc, pc, pc)
    return res if ctx_out else (res[0], None)


FF_COLS = 1024


def _out_mlp_kernel(x_ref, yr_ref, yg_ref, yw_ref, yh_ref, mod_ref, wo_ref, wu_ref, wd_ref,
                    g1_ref, b1_ref, g2_ref, b2_ref, o_ref, h_scr):
    half = x_ref.shape[1] // 2
    n_ff = wu_ref.shape[1] // FF_COLS
    halves = (slice(0, half), slice(half, 2 * half))

    def mixer_proj(r):
        y = None
        for s, y_ref in enumerate((yr_ref, yg_ref, yw_ref, yh_ref)):
            part = _dot(y_ref[0, r, :], wo_ref[s * GROUP_W:(s + 1) * GROUP_W, :])
            y = part if y is None else y + part
        return y

    def post_mixer(r, y):
        x1 = _ln_rows(ALPHA * x_ref[0, r, :] + mod_ref[0, 2:3, :] * y) * g1_ref[...] + b1_ref[...]
        o_ref[0, r, :] = x1
        h_scr[r, :] = (_ln_rows(x1) * (1.0 + mod_ref[0, 4:5, :]) + mod_ref[0, 3:4, :]).astype(BF16)

    def up(r, f):
        return _dot(h_scr[r, :], wu_ref[:, f * FF_COLS:(f + 1) * FF_COLS])

    def down(u, f):
        u = jnp.maximum(u, 0.0)
        return _dot((u * u).astype(BF16), wd_ref[f * FF_COLS:(f + 1) * FF_COLS, :])

    def post_mlp(r, acc):
        o_ref[0, r, :] = _ln_rows(ALPHA * o_ref[0, r, :] + mod_ref[0, 5:6, :] * acc) * g2_ref[...] + b2_ref[...]

    ys = [mixer_proj(r) for r in halves]
    post_mixer(halves[0], ys[0])
    units = [(r, f) for r in halves for f in range(n_ff)]
    u_next = up(*units[0])
    post_mixer(halves[1], ys[1])
    acc = None
    for k, (r, f) in enumerate(units):
        u = u_next
        if k + 1 < len(units):
            u_next = up(*units[k + 1])
        part = down(u, f)
        acc = part if f == 0 else acc + part
        if f == n_ff - 1:
            post_mlp(r, acc)


def _out_mlp(x, ys, mod, w_out, w_up, w_down, g1, b1, g2, b2, tm, shared_mod):
    b, n, d = x.shape
    mod_map = (lambda i, t: (0, 0, 0)) if shared_mod else (lambda i, t: (i, 0, 0))
    full = lambda a: pl.BlockSpec(a.shape, lambda i, t: (0,) * a.ndim, pipeline_mode=pl.Buffered(1))
    vecs = [v.reshape(1, d) for v in (g1, b1, g2, b2)]
    return pl.pallas_call(
        _out_mlp_kernel,
        grid=(b, n // tm),
        in_specs=[pl.BlockSpec((1, tm, d), lambda i, t: (i, t, 0))]
        + [pl.BlockSpec((1, tm, GROUP_W), lambda i, t: (i, t, 0))] * 4
        + [pl.BlockSpec((1, 6, d), mod_map), full(w_out), full(w_up), full(w_down)]
        + [full(v) for v in vecs],
        out_specs=pl.BlockSpec((1, tm, d), lambda i, t: (i, t, 0)),
        out_shape=jax.ShapeDtypeStruct((b, n, d), F32),
        scratch_shapes=[pltpu.VMEM((tm, d), BF16)],
        compiler_params=pltpu.CompilerParams(dimension_semantics=("arbitrary", "arbitrary"),
                                             vmem_limit_bytes=VMEM_LIMIT),
        name="out_proj_mlp",
    )(x, *ys, mod, w_out, w_up, w_down, *vecs)


def _rope_tables(n):
    rows = n // GRID_W
    row = jnp.repeat(jnp.arange(rows), GRID_W).astype(F32)
    col = (jnp.arange(rows * GRID_W) % GRID_W).astype(F32)
    inv = ROPE_THETA ** (-jnp.arange(ROPE_FREQS, dtype=F32) / ROPE_FREQS)
    cos, sin = [], []
    for pos in (row, col):
        ang = pos[:, None] * inv
        cos += [jnp.cos(ang), jnp.cos(ang)]
        sin += [-jnp.sin(ang), jnp.sin(ang)]
    return tuple(jnp.tile(jnp.concatenate(t, axis=1), (1, LANES // HEAD_DIM)) for t in (cos, sin))


def kernel(x, c, ctx, c_ctx, w_ada, b_ada, w_in, ret_decay_logit, gqa_q_gain, gqa_k_gain,
           swa_sink, hgrn_lb, w_out, ln1_g, ln1_b, w_up, w_down, ln2_g, ln2_b):
    batch, n, d = x.shape
    depth = w_ada.shape[0]
    rope_tabs = _rope_tables(n)
    p_lb = jax.nn.softmax(hgrn_lb.astype(F32), axis=0)
    lower_bounds = jnp.cumsum(p_lb, axis=0) - p_lb[0]
    log_decay = jax.nn.log_sigmoid(ret_decay_logit.astype(F32))

    pad = (-(batch + 1)) % 8
    cvec = jnp.concatenate([c, c_ctx[None, :], jnp.zeros((pad, d), F32)], axis=0)
    mods = _modulation(cvec, w_ada, b_ada)
    w_in_bf, w_out_bf, w_up_bf, w_down_bf = (w.astype(BF16) for w in (w_in, w_out, w_up, w_down))

    xc = ctx
    for l in range(depth):
        ctx_out = l < depth - 1
        m_lat = mods[l, :batch].reshape(batch, 6, d)
        m_ctx = mods[l, batch:batch + 1].reshape(1, 6, d)
        p = _project(x, m_lat, w_in_bf[l], 1024, False)
        pc = _project(xc, m_ctx, w_in_bf[l], xc.shape[1], True)
        y_ret, yc_ret = _retention(p[0], pc[0], log_decay[l], rope_tabs, ctx_out)
        y_glb, yc_glb = _global_gqa(p[1], pc[1], gqa_q_gain[l], gqa_k_gain[l], rope_tabs, ctx_out)
        y_win, yc_win = _window_gqa(p[2], pc[2], swa_sink[l], rope_tabs, ctx_out)
        y_hg, yc_hg = _hgrn2(p[3], pc[3], lower_bounds[l], ctx_out)
        post = (w_out_bf[l], w_up_bf[l], w_down_bf[l], ln1_g[l], ln1_b[l], ln2_g[l], ln2_b[l])
        x = _out_mlp(x, (y_ret, y_glb, y_win, y_hg), m_lat, *post, 1024, False)
        if ctx_out:
            xc = _out_mlp(xc, (yc_ret, yc_glb, yc_win, yc_hg), m_ctx, *post, 256, True)
    return x
```
